```python
import math
import jax, jax.numpy as jnp
from jax import lax
import numpy as np

D_MODEL = 2048
BATCH = 1
SEQ = 16384
DEPTH = 1
DEC_BATCH = 32
DEC_SEQ = 64
PAST_LEN = 2048

CHUNK = 64
ATT_WIDTH = D_MODEL // 2
SSM_WIDTH = D_MODEL - ATT_WIDTH
ATT_HEADS = 8
V_DIM = ATT_WIDTH // ATT_HEADS
QK_DIM = V_DIM // 2
Q_BLOCK = 128
SSM_GROUP = 16
SSM_GROUPS = SSM_WIDTH // SSM_GROUP
SSM_STATE = 64
PEER_HEADS = 8
PEER_NKEYS = 128
PEER_EXPERTS = PEER_NKEYS * PEER_NKEYS
PEER_DKEY = 256
PEER_TOPK = 16
PEER_BLOCK = 128
NORM_EPS = 1e-6

kernel_name = 'hymba_diffattn_s5_peer_stream_step'


def _rms(x, g):
    xf = x.astype(jnp.float32)
    y = xf * lax.rsqrt(jnp.mean(xf * xf, axis=-1, keepdims=True) + NORM_EPS)
    return (y * g.astype(jnp.float32)).astype(x.dtype)


def _diff_attn_block(q, q_pos, k, v, k_pos, lam):
    s = jnp.einsum('bqhcd,bkhcd->bchqk', q, k).astype(jnp.float32) * (QK_DIM ** -0.5)
    mask = (k_pos[None, :] // CHUNK) <= (q_pos[:, None] // CHUNK)
    p = jax.nn.softmax(jnp.where(mask, s, -jnp.inf), axis=-1)
    a = p[:, 0] - lam * p[:, 1]
    return jnp.einsum('bhqk,bkhe->bqhe', a.astype(v.dtype), v)


def _attn_prompt(q, k, v, lam):
    b, s = q.shape[0], q.shape[1]
    nb = s // Q_BLOCK
    qb = q.reshape(b, nb, Q_BLOCK, ATT_HEADS, 2, QK_DIM).swapaxes(0, 1)
    k_pos = jnp.arange(s)

    def blk(args):
        q_blk, i = args
        q_pos = i * Q_BLOCK + jnp.arange(Q_BLOCK)
        return _diff_attn_block(q_blk, q_pos, k, v, k_pos, lam)

    out = lax.map(blk, (qb, jnp.arange(nb)))
    return out.swapaxes(0, 1).reshape(b, s, ATT_HEADS, V_DIM)


def _cmul(ar, ai, br, bi):
    return ar * br - ai * bi, ar * bi + ai * br


def _ssm_combine(e1, e2):
    a1r, a1i, b1r, b1i = e1
    a2r, a2i, b2r, b2i = e2
    ar, ai = _cmul(a2r, a2i, a1r, a1i)
    br, bi = _cmul(a2r, a2i, b1r, b1i)
    return ar, ai, br + b2r, bi + b2i


def _s5(u, h0_re, h0_im, a_re, a_im, log_dt, b_re, b_im, c_re, c_im, d_skip, w_glu, b_glu):
    f32 = jnp.float32
    b, s = u.shape[0], u.shape[1]
    ug = u.astype(f32).reshape(b, s, SSM_GROUPS, SSM_GROUP)
    dt = jnp.exp(log_dt.astype(f32))[:, None]
    ar, ai = a_re.astype(f32), a_im.astype(f32)
    mag = jnp.exp(ar * dt)
    abar_re, abar_im = mag * jnp.cos(ai * dt), mag * jnp.sin(ai * dt)
    den = ar * ar + ai * ai
    nr, ni = abar_re - 1.0, abar_im
    f_re, f_im = (nr * ar + ni * ai) / den, (ni * ar - nr * ai) / den
    br, bi = b_re.astype(f32), b_im.astype(f32)
    bbar_re = f_re[..., None] * br - f_im[..., None] * bi
    bbar_im = f_re[..., None] * bi + f_im[..., None] * br
    bu_re = jnp.einsum('gpc,bsgc->bsgp', bbar_re, ug)
    bu_im = jnp.einsum('gpc,bsgc->bsgp', bbar_im, ug)
    a_el_re = jnp.broadcast_to(abar_re, bu_re.shape)
    a_el_im = jnp.broadcast_to(abar_im, bu_im.shape)
    acum_re, acum_im, h_re, h_im = lax.associative_scan(
        _ssm_combine, (a_el_re, a_el_im, bu_re, bu_im), axis=1)
    hr0, hi0 = _cmul(acum_re, acum_im, h0_re[:, None], h0_im[:, None])
    h_re, h_im = h_re + hr0, h_im + hi0
    y = (jnp.einsum('gcp,bsgp->bsgc', c_re.astype(f32), h_re)
         - jnp.einsum('gcp,bsgp->bsgc', c_im.astype(f32), h_im)
         + d_skip.astype(f32).reshape(SSM_GROUPS, SSM_GROUP) * ug)
    z = jax.nn.gelu(y.reshape(b, s, SSM_WIDTH), approximate=False)
    z = z * jax.nn.sigmoid(z @ w_glu.astype(f32) + b_glu.astype(f32))
    return z.astype(u.dtype), h_re[:, -1], h_im[:, -1]


def _peer(x, w_query, sub_keys, u_table, v_table):
    lead = x.shape[:-1]
    xt = x.reshape(-1, D_MODEL)
    t = xt.shape[0]
    nb = -(-t // PEER_BLOCK)
    xt = jnp.pad(xt, ((0, nb * PEER_BLOCK - t), (0, 0))).reshape(nb, PEER_BLOCK, D_MODEL)

    def blk(xb):
        q = (xb @ w_query).reshape(PEER_BLOCK, PEER_HEADS, 2, PEER_DKEY // 2)
        sc = jnp.einsum('thcd,hcnd->thcn', q, sub_keys).astype(jnp.float32)
        sv, si = lax.top_k(sc, PEER_TOPK)
        cand = (sv[:, :, 0, :, None] + sv[:, :, 1, None, :]).reshape(PEER_BLOCK, PEER_HEADS, PEER_TOPK * PEER_TOPK)
        cidx = (si[:, :, 0, :, None] * PEER_NKEYS + si[:, :, 1, None, :]).reshape(PEER_BLOCK, PEER_HEADS, PEER_TOPK * PEER_TOPK)
        fv, fpos = lax.top_k(cand, PEER_TOPK)
        eidx = jnp.take_along_axis(cidx, fpos, axis=-1)
        gate = jax.nn.softmax(fv, axis=-1)
        hid = jnp.einsum('td,thkd->thk', xb, u_table[eidx])
        w = gate.astype(xb.dtype) * jax.nn.gelu(hid, approximate=False)
        return jnp.einsum('thk,thkd->td', w, v_table[eidx])

    out = lax.map(blk, xt).reshape(nb * PEER_BLOCK, D_MODEL)[:t]
    return out.reshape(*lead, D_MODEL)


def _layer(x, k_past, v_past, h0_re, h0_im, lambda_init, g_mix, w_in, g_q, g_k, lam_q1, lam_k1,
           lam_q2, lam_k2, g_sub, a_re, a_im, log_dt, b_re, b_im, c_re, c_im, d_skip, w_glu,
           b_glu, g_ssm, w_out, g_ffn, w_query, sub_keys, u_table, v_table):
    f32 = jnp.float32
    b, s = x.shape[0], x.shape[1]
    xn = _rms(x, g_mix)
    q, k, v, u = jnp.split(xn @ w_in, [ATT_WIDTH, 2 * ATT_WIDTH, 3 * ATT_WIDTH], axis=-1)
    q = _rms(q.reshape(b, s, ATT_HEADS, 2, QK_DIM), g_q)
    k = _rms(k.reshape(b, s, ATT_HEADS, 2, QK_DIM), g_k)
    v = v.reshape(b, s, ATT_HEADS, V_DIM)
    lam = (jnp.exp(jnp.sum(lam_q1.astype(f32) * lam_k1.astype(f32)))
           - jnp.exp(jnp.sum(lam_q2.astype(f32) * lam_k2.astype(f32))) + lambda_init)
    if k_past is None:
        o = _attn_prompt(q, k, v, lam)
        h0_re = jnp.zeros((b, SSM_GROUPS, SSM_STATE), f32)
        h0_im = h0_re
    else:
        past = k_past.shape[1]
        k_all = jnp.concatenate([k_past.astype(k.dtype), k], axis=1)
        v_all = jnp.concatenate([v_past.astype(v.dtype), v], axis=1)
        o = _diff_attn_block(q, past + jnp.arange(s), k_all, v_all, jnp.arange(past + s), lam)
    o = (_rms(o, g_sub) * (1.0 - lambda_init)).reshape(b, s, ATT_WIDTH)
    y_ssm, h_re, h_im = _s5(u, h0_re.astype(f32), h0_im.astype(f32), a_re, a_im, log_dt, b_re, b_im,
                            c_re, c_im, d_skip, w_glu, b_glu)
    y_ssm = _rms(y_ssm, g_ssm)
    h = x + jnp.concatenate([o.astype(x.dtype), y_ssm.astype(x.dtype)], axis=-1) @ w_out
    y = h + _peer(_rms(h, g_ffn), w_query, sub_keys, u_table, v_table)
    return y, k, v, h_re, h_im


def setup_inputs(seed: int = 0) -> dict:
    key = jax.random.key(seed)
    ks = jax.random.split(key, 40)
    f32 = jnp.float32
    L = DEPTH
    n = jnp.arange(SSM_STATE, dtype=f32)

    def nrm(i, shape, scale):
        return scale * jax.random.normal(ks[i], shape, f32)

    def gain(i, shape):
        return 1.0 + 0.02 * jax.random.normal(ks[i], shape, f32)

    return {
        'x_prompt': nrm(0, (BATCH, SEQ, D_MODEL), 1.0),
        'x_sample': nrm(1, (DEC_BATCH, DEC_SEQ, D_MODEL), 1.0),
        'cache_k': nrm(2, (L, DEC_BATCH, PAST_LEN, ATT_HEADS, 2, QK_DIM), 1.0),
        'cache_v': nrm(3, (L, DEC_BATCH, PAST_LEN, ATT_HEADS, V_DIM), 1.0),
        'state_ssm_re': nrm(4, (L, DEC_BATCH, SSM_GROUPS, SSM_STATE), 0.5),
        'state_ssm_im': nrm(5, (L, DEC_BATCH, SSM_GROUPS, SSM_STATE), 0.5),
        'g_mix': gain(6, (L, D_MODEL)),
        'w_in': nrm(7, (L, D_MODEL, 3 * ATT_WIDTH + SSM_WIDTH), D_MODEL ** -0.5),
        'g_q': gain(8, (L, QK_DIM)),
        'g_k': gain(9, (L, QK_DIM)),
        'lam_q1': nrm(10, (L, QK_DIM), 0.1),
        'lam_k1': nrm(11, (L, QK_DIM), 0.1),
        'lam_q2': nrm(12, (L, QK_DIM), 0.1),
        'lam_k2': nrm(13, (L, QK_DIM), 0.1),
        'g_sub': gain(14, (L, V_DIM)),
        'a_re': -0.5 + nrm(15, (L, SSM_GROUPS, SSM_STATE), 0.01),
        'a_im': math.pi * n + nrm(16, (L, SSM_GROUPS, SSM_STATE), 0.01),
        'log_dt': jax.random.uniform(ks[17], (L, SSM_GROUPS), f32, math.log(1e-3), math.log(1e-1)),
        'b_re': nrm(18, (L, SSM_GROUPS, SSM_STATE, SSM_GROUP), (2 * SSM_GROUP) ** -0.5),
        'b_im': nrm(19, (L, SSM_GROUPS, SSM_STATE, SSM_GROUP), (2 * SSM_GROUP) ** -0.5),
        'c_re': nrm(20, (L, SSM_GROUPS, SSM_GROUP, SSM_STATE), (2 * SSM_STATE) ** -0.5),
        'c_im': nrm(21, (L, SSM_GROUPS, SSM_GROUP, SSM_STATE), (2 * SSM_STATE) ** -0.5),
        'd_skip': nrm(22, (L, SSM_WIDTH), 1.0),
        'w_glu': nrm(23, (L, SSM_WIDTH, SSM_WIDTH), SSM_WIDTH ** -0.5),
        'b_glu': nrm(24, (L, SSM_WIDTH), 0.02),
        'g_ssm': gain(25, (L, SSM_WIDTH)),
        'w_out': nrm(26, (L, D_MODEL, D_MODEL), D_MODEL ** -0.5),
        'g_ffn': gain(27, (L, D_MODEL)),
        'w_query': nrm(28, (L, D_MODEL, PEER_HEADS * PEER_DKEY), D_MODEL ** -0.5),
        'sub_keys': nrm(29, (L, PEER_HEADS, 2, PEER_NKEYS, PEER_DKEY // 2), (PEER_DKEY // 2) ** -0.5),
        'u_table': nrm(30, (L, PEER_EXPERTS, D_MODEL), D_MODEL ** -0.5),
        'v_table': nrm(31, (L, PEER_EXPERTS, D_MODEL), PEER_HEADS ** -0.5),
    }


def reference(x_prompt, x_sample, cache_k, cache_v, state_ssm_re, state_ssm_im, g_mix, w_in, g_q,
              g_k, lam_q1, lam_k1, lam_q2, lam_k2, g_sub, a_re, a_im, log_dt, b_re, b_im, c_re,
              c_im, d_skip, w_glu, b_glu, g_ssm, w_out, g_ffn, w_query, sub_keys, u_table, v_table):
    yp, ys = x_prompt, x_sample
    kp_l, vp_l, hrp_l, hip_l = [], [], [], []
    ks_l, vs_l, hrs_l, his_l = [], [], [], []
    for l in range(DEPTH):
        lambda_init = 0.8 - 0.6 * math.exp(-0.3 * l)
        p = dict(g_mix=g_mix[l], w_in=w_in[l], g_q=g_q[l], g_k=g_k[l], lam_q1=lam_q1[l],
                 lam_k1=lam_k1[l], lam_q2=lam_q2[l], lam_k2=lam_k2[l], g_sub=g_sub[l],
                 a_re=a_re[l], a_im=a_im[l], log_dt=log_dt[l], b_re=b_re[l], b_im=b_im[l],
                 c_re=c_re[l], c_im=c_im[l], d_skip=d_skip[l], w_glu=w_glu[l], b_glu=b_glu[l],
                 g_ssm=g_ssm[l], w_out=w_out[l], g_ffn=g_ffn[l], w_query=w_query[l],
                 sub_keys=sub_keys[l], u_table=u_table[l], v_table=v_table[l])
        yp, kp, vp, hrp, hip = _layer(yp, None, None, None, None, lambda_init, **p)
        ys, kk, vv, hrs, his = _layer(ys, cache_k[l], cache_v[l], state_ssm_re[l], state_ssm_im[l],
                                      lambda_init, **p)
        kp_l.append(kp); vp_l.append(vp); hrp_l.append(hrp); hip_l.append(hip)
        ks_l.append(kk); vs_l.append(vv); hrs_l.append(hrs); his_l.append(his)
    return (yp, ys, jnp.stack(kp_l), jnp.stack(vp_l), jnp.stack(hrp_l), jnp.stack(hip_l),
            jnp.stack(ks_l), jnp.stack(vs_l), jnp.stack(hrs_l), jnp.stack(his_l))
```

```python
import functools
import math

import jax
import jax.numpy as jnp
from jax import lax
from jax.experimental import pallas as pl
from jax.experimental.pallas import tpu as pltpu

F32 = jnp.float32
BF16 = jnp.bfloat16

CHUNK = 64
ATT_HEADS = 8
QK_DIM = 64
V_DIM = 2 * QK_DIM
SSM_GROUP = 16
SSM_STATE = 64
PEER_HEADS = 8
PEER_NKEYS = 128
PEER_TOPK = 16
NORM_EPS = 1e-6

LANES = 128
SUBLANES = 8
SSM_SUPER = LANES // SSM_GROUP
SSM_SLAB = SSM_SUPER * SSM_STATE
VMEM_LIMIT = 56 * 1024 * 1024
SQRT_HALF = math.sqrt(0.5)

_NT = (((1,), (1,)), ((), ()))


def _params(*sem):
    return pltpu.CompilerParams(dimension_semantics=sem, vmem_limit_bytes=VMEM_LIMIT)


def _const_spec(shape):
    nd = len(shape)
    return pl.BlockSpec(shape, lambda *_: (0,) * nd, pipeline_mode=pl.Buffered(1))


def _gelu(x):
    return 0.5 * x * (1.0 + lax.erf(x * SQRT_HALF))


def _dot(a, b):
    return jnp.dot(a, b, preferred_element_type=F32)


def _in_proj_body(x_ref, gmix_ref, w_ref, seg_ref, gq_ref, gk_ref,
                  qb_ref, k_ref, kb_ref, v_ref, vb_ref, u_ref):
    aw = k_ref.shape[-1]
    x = x_ref[...]
    ms = jnp.mean(x * x, axis=-1, keepdims=True)
    xn = (x * lax.rsqrt(ms + NORM_EPS) * gmix_ref[...]).astype(BF16)

    def seg_rms(y, g):
        sq = y * y
        hi = sq.astype(BF16)
        lo = (sq - hi.astype(F32)).astype(BF16)
        mean = _dot(hi, seg_ref[...]) + _dot(lo, seg_ref[...])
        return y * lax.rsqrt(mean + NORM_EPS) * g

    q = seg_rms(_dot(xn, w_ref[:, 0:aw]), gq_ref[...])
    qb_ref[...] = q.astype(BF16)
    k = seg_rms(_dot(xn, w_ref[:, aw:2 * aw]), gk_ref[...])
    k_ref[...] = k
    kb_ref[...] = k.astype(BF16)
    v = _dot(xn, w_ref[:, 2 * aw:3 * aw])
    v_ref[...] = v
    vb_ref[...] = v.astype(BF16)
    u_ref[...] = _dot(xn, w_ref[:, 3 * aw:])


def _in_proj(x2, g_mix, w_in_b, seg, gq_t, gk_t, tb):
    t, d = x2.shape
    n = w_in_b.shape[1]
    aw = seg.shape[0]
    sw = n - 3 * aw
    tb = min(tb, t)
    row = lambda w: pl.BlockSpec((tb, w), lambda i: (i, 0))
    return pl.pallas_call(
        _in_proj_body,
        grid=(t // tb,),
        in_specs=[row(d), _const_spec((1, d)), _const_spec((d, n)), _const_spec((aw, aw)),
                  _const_spec((1, aw)), _const_spec((1, aw))],
        out_specs=[row(aw), row(aw), row(aw), row(aw), row(aw), row(sw)],
        out_shape=[jax.ShapeDtypeStruct((t, aw), BF16), jax.ShapeDtypeStruct((t, aw), F32),
                   jax.ShapeDtypeStruct((t, aw), BF16), jax.ShapeDtypeStruct((t, aw), F32),
                   jax.ShapeDtypeStruct((t, aw), BF16), jax.ShapeDtypeStruct((t, sw), F32)],
        compiler_params=_params("parallel"),
        name="in_proj",
    )(x2, g_mix, w_in_b, seg, gq_t, gk_t)


def _lambda(lq1_ref, lk1_ref, lq2_ref, lk2_ref, lambda_init):
    s1 = jnp.sum(lq1_ref[...] * lk1_ref[...], axis=-1, keepdims=True)
    s2 = jnp.sum(lq2_ref[...] * lk2_ref[...], axis=-1, keepdims=True)
    return jnp.exp(s1) - jnp.exp(s2) + lambda_init


def _stack_maps(q):
    lane = lax.broadcasted_iota(jnp.int32, q.shape, 1)
    zero = jnp.zeros_like(q)
    return jnp.concatenate([jnp.where(lane < QK_DIM, q, zero), jnp.where(lane >= QK_DIM, q, zero)], axis=0)


def _sub_norm(o2, l, lam, gsub, nq, lambda_init):
    o2 = o2 / l
    o = o2[:nq] - lam * o2[nq:]
    ms = jnp.mean(o * o, axis=-1, keepdims=True)
    return (o * lax.rsqrt(ms + NORM_EPS) * gsub) * (1.0 - lambda_init)


def _attn_prompt_body(lq1_ref, lk1_ref, lq2_ref, lk2_ref, gsub_ref, q_ref, k_ref, v_ref, o_ref,
                      q2_scr, m_scr, l_scr, acc_scr, *, tq, tk, lambda_init):
    i = pl.program_id(1)
    q2_scr[...] = _stack_maps(q_ref[...])
    m_scr[...] = jnp.full(m_scr.shape, -jnp.inf, F32)
    l_scr[...] = jnp.zeros(l_scr.shape, F32)
    acc_scr[...] = jnp.zeros(acc_scr.shape, F32)

    def step(j, masked):
        rows = pl.ds(pl.multiple_of(j * tk, tk), tk)
        s = lax.dot_general(q2_scr[...], k_ref[rows, :], _NT, preferred_element_type=F32)
        if masked:
            row = lax.broadcasted_iota(jnp.int32, s.shape, 0)
            col = lax.broadcasted_iota(jnp.int32, s.shape, 1)
            q_chunk = (i * tq + jnp.where(row >= tq, row - tq, row)) // CHUNK
            k_chunk = (j * tk + col) // CHUNK
            s = jnp.where(k_chunk <= q_chunk, s, -jnp.inf)
        m_prev = m_scr[...]
        m_new = jnp.maximum(m_prev, jnp.max(s, axis=1, keepdims=True))
        alpha = jnp.exp(m_prev - m_new)
        p = jnp.exp(s - m_new)
        l_scr[...] = alpha * l_scr[...] + jnp.sum(p, axis=1, keepdims=True)
        acc_scr[...] = alpha * acc_scr[...] + _dot(p.astype(BF16), v_ref[rows, :])
        m_scr[...] = m_new

    jd = (i * tq) // tk

    def full_step(j, c):
        step(j, False)
        return c

    lax.fori_loop(0, jd, full_step, 0)
    step(jd, True)
    lam = _lambda(lq1_ref, lk1_ref, lq2_ref, lk2_ref, lambda_init)
    o_ref[...] = _sub_norm(acc_scr[...], l_scr[...], lam, gsub_ref[...], tq, lambda_init)


def _attn_prompt(qb, kb, vb, lam_parts, g_sub, lambda_init, tq, tk):
    s, aw = qb.shape
    tk = min(tk, s)
    tq = min(tq, tk)
    assert s % tk == 0 and tk % tq == 0 and tq % CHUNK == 0
    lam_specs = [_const_spec((1, QK_DIM))] * 4
    head_all = pl.BlockSpec((s, V_DIM), lambda h, i: (0, h))
    return pl.pallas_call(
        functools.partial(_attn_prompt_body, tq=tq, tk=tk, lambda_init=lambda_init),
        grid=(ATT_HEADS, s // tq),
        in_specs=lam_specs + [_const_spec((1, V_DIM)),
                              pl.BlockSpec((tq, V_DIM), lambda h, i: (i, h)), head_all, head_all],
        out_specs=pl.BlockSpec((tq, V_DIM), lambda h, i: (i, h)),
        out_shape=jax.ShapeDtypeStruct((s, aw), F32),
        scratch_shapes=[pltpu.VMEM((2 * tq, V_DIM), BF16), pltpu.VMEM((2 * tq, 1), F32),
                        pltpu.VMEM((2 * tq, 1), F32), pltpu.VMEM((2 * tq, V_DIM), F32)],
        compiler_params=_params("parallel", "arbitrary"),
        name="attn_prompt",
    )(*lam_parts, g_sub, qb, kb, vb)


def _attn_sample_body(lq1_ref, lk1_ref, lq2_ref, lk2_ref, gsub_ref, q_ref, kp_ref, vp_ref, kn_ref, vn_ref,
                      o_ref, *, past, lambda_init):
    nq = q_ref.shape[1]
    q2 = _stack_maps(q_ref[0])
    s_past = lax.dot_general(q2, kp_ref[0].astype(BF16), _NT, preferred_element_type=F32)
    s_new = lax.dot_general(q2, kn_ref[0], _NT, preferred_element_type=F32)
    row = lax.broadcasted_iota(jnp.int32, s_new.shape, 0)
    col = lax.broadcasted_iota(jnp.int32, s_new.shape, 1)
    q_chunk = (past + jnp.where(row >= nq, row - nq, row)) // CHUNK
    s_new = jnp.where((past + col) // CHUNK <= q_chunk, s_new, -jnp.inf)
    m = jnp.maximum(jnp.max(s_past, axis=1, keepdims=True), jnp.max(s_new, axis=1, keepdims=True))
    p_past = jnp.exp(s_past - m)
    p_new = jnp.exp(s_new - m)
    l = jnp.sum(p_past, axis=1, keepdims=True) + jnp.sum(p_new, axis=1, keepdims=True)
    acc = _dot(p_past.astype(BF16), vp_ref[0].astype(BF16)) + _dot(p_new.astype(BF16), vn_ref[0])
    lam = _lambda(lq1_ref, lk1_ref, lq2_ref, lk2_ref, lambda_init)
    o_ref[0] = _sub_norm(acc, l, lam, gsub_ref[...], nq, lambda_init)


def _attn_sample(qb3, kp3, vp3, kb3, vb3, lam_parts, g_sub, lambda_init):
    b, s, aw = qb3.shape
    past = kp3.shape[1]
    assert past % CHUNK == 0
    lam_specs = [_const_spec((1, QK_DIM))] * 4
    new = pl.BlockSpec((1, s, V_DIM), lambda bi, h: (bi, 0, h))
    old = pl.BlockSpec((1, past, V_DIM), lambda bi, h: (bi, 0, h))
    return pl.pallas_call(
        functools.partial(_attn_sample_body, past=past, lambda_init=lambda_init),
        grid=(b, ATT_HEADS),
        in_specs=lam_specs + [_const_spec((1, V_DIM)), new, old, old, new, new],
        out_specs=new,
        out_shape=jax.ShapeDtypeStruct((b, s, aw), F32),
        compiler_params=_params("parallel", "arbitrary"),
        name="attn_sample",
    )(*lam_parts, g_sub, qb3, kp3, vp3, kb3, vb3)


def _ssm_param_body(are_ref, aim_ref, ldt_ref, bre_ref, bim_ref, abr_ref, abi_ref, bbr_ref, bbi_ref):
    ar, ai = are_ref[...], aim_ref[...]
    dt = jnp.exp(ldt_ref[...])
    mag = jnp.exp(ar * dt)
    abr, abi = mag * jnp.cos(ai * dt), mag * jnp.sin(ai * dt)
    den = ar * ar + ai * ai
    nr, ni = abr - 1.0, abi
    fr, fi = (nr * ar + ni * ai) / den, (ni * ar - nr * ai) / den
    br, bi = bre_ref[...], bim_ref[...]
    abr_ref[...] = abr
    abi_ref[...] = abi
    bbr_ref[...] = fr * br - fi * bi
    bbi_ref[...] = fr * bi + fi * br


def _ssm_params(a_re, a_im, log_dt, b_re, b_im, c_re, c_im):
    g, p = a_re.shape
    c = b_re.shape[-1]
    n = g * p
    col = lambda a: a.reshape(n, 1)
    abr, abi, bbr, bbi = pl.pallas_call(
        _ssm_param_body,
        out_shape=[jax.ShapeDtypeStruct((n, 1), F32)] * 2 + [jax.ShapeDtypeStruct((n, c), F32)] * 2,
        name="ssm_params",
    )(col(a_re), col(a_im), col(jnp.repeat(log_dt, p)), b_re.reshape(n, c), b_im.reshape(n, c))
    ns = g // SSM_SUPER
    eye = jnp.eye(SSM_SUPER, dtype=F32)

    def in_blocks(bb):
        bb = bb.reshape(ns, SSM_SUPER, p, c)
        return jnp.einsum('sgpc,gh->sgchp', bb, eye).reshape(ns, SSM_SUPER * c, SSM_SLAB)

    def out_blocks(cc):
        cc = cc.reshape(ns, SSM_SUPER, c, p)
        return jnp.einsum('sgcp,gh->sgphc', cc, eye).reshape(ns, SSM_SLAB, SSM_SUPER * c)

    bw = jnp.concatenate([in_blocks(bbr), in_blocks(bbi)], axis=2).astype(BF16)
    cw = jnp.concatenate([out_blocks(c_re), -out_blocks(c_im)], axis=1).astype(BF16)
    pr, pi = [abr.reshape(ns, 1, SSM_SLAB)], [abi.reshape(ns, 1, SSM_SLAB)]
    for _ in range(SUBLANES - 1):
        pr, pi = (pr + [pr[-1] * pr[0] - pi[-1] * pi[0]], pi + [pr[-1] * pi[0] + pi[-1] * pr[0]])
    rows = jnp.arange(SUBLANES).reshape(1, SUBLANES, 1)
    tabs = []
    for k in (1, 2, 4):
        tabs += [jnp.where(rows >= k, pr[k - 1], 0.0), jnp.where(rows >= k, pi[k - 1], 0.0)]
    tabs += [jnp.concatenate(pr, axis=1), jnp.concatenate(pi, axis=1)]
    return bw, cw, jnp.stack(tabs, axis=1)


def _ssm_body(u_ref, bw_ref, cw_ref, pw_ref, dsk_ref, h0r_ref, h0i_ref, y_ref, hr_ref, hi_ref,
              h_scr, c_scr):
    tb = pl.program_id(2)
    nrow = u_ref.shape[1]
    ns = SSM_SLAB

    @pl.when(tb == 0)
    def _():
        c_scr[0:1, :] = h0r_ref[0]
        c_scr[1:2, :] = h0i_ref[0]

    u = u_ref[0]
    h_scr[...] = _dot(u.astype(BF16), bw_ref[0])
    t1r, t1i, t2r, t2i, t4r, t4i, tcr, tci = (pw_ref[0, n] for n in range(8))

    def body(r, carry):
        cr, ci = carry
        rows = pl.ds(pl.multiple_of(r * SUBLANES, SUBLANES), SUBLANES)
        xr, xi = h_scr[rows, 0:ns], h_scr[rows, ns:2 * ns]
        for k, pr, pi in ((1, t1r, t1i), (2, t2r, t2i), (4, t4r, t4i)):
            sr, si = pltpu.roll(xr, k, 0), pltpu.roll(xi, k, 0)
            xr, xi = xr + pr * sr - pi * si, xi + pr * si + pi * sr
        xr, xi = xr + tcr * cr - tci * ci, xi + tcr * ci + tci * cr
        h_scr[rows, 0:ns] = xr
        h_scr[rows, ns:2 * ns] = xi
        return xr[SUBLANES - 1:SUBLANES], xi[SUBLANES - 1:SUBLANES]

    cr, ci = lax.fori_loop(0, nrow // SUBLANES, body, (c_scr[0:1, :], c_scr[1:2, :]))
    c_scr[0:1, :] = cr
    c_scr[1:2, :] = ci
    y_ref[0] = _dot(h_scr[...].astype(BF16), cw_ref[0]) + dsk_ref[...] * u

    @pl.when(tb == pl.num_programs(2) - 1)
    def _():
        hr_ref[0] = cr
        hi_ref[0] = ci


def _ssm(u3, h0r, h0i, bw, cw, pw, d_skip, tb):
    b, s, sw = u3.shape
    ns = sw // LANES
    tb = min(tb, s)
    assert s % tb == 0 and tb % SUBLANES == 0
    nstate = ns * SSM_SLAB
    slab = lambda r, c: pl.BlockSpec((1, r, c), lambda bi, g, t: (g, 0, 0))
    state = pl.BlockSpec((1, 1, SSM_SLAB), lambda bi, g, t: (bi, 0, g))
    seq = pl.BlockSpec((1, tb, LANES), lambda bi, g, t: (bi, t, g))
    return pl.pallas_call(
        _ssm_body,
        grid=(b, ns, s // tb),
        in_specs=[seq, slab(LANES, 2 * SSM_SLAB), slab(2 * SSM_SLAB, LANES),
                  pl.BlockSpec((1, 8, SUBLANES, SSM_SLAB), lambda bi, g, t: (g, 0, 0, 0)),
                  pl.BlockSpec((1, LANES), lambda bi, g, t: (0, g)), state, state],
        out_specs=[seq, state, state],
        out_shape=[jax.ShapeDtypeStruct((b, s, sw), F32), jax.ShapeDtypeStruct((b, 1, nstate), F32),
                   jax.ShapeDtypeStruct((b, 1, nstate), F32)],
        scratch_shapes=[pltpu.VMEM((tb, 2 * SSM_SLAB), F32), pltpu.VMEM((SUBLANES, SSM_SLAB), F32)],
        compiler_params=_params("parallel", "parallel", "arbitrary"),
        name="ssm",
    )(u3, bw, cw, pw, d_skip, h0r.reshape(b, 1, nstate), h0i.reshape(b, 1, nstate))


def _post_body(y_ref, o_ref, x_ref, wglu_ref, bglu_ref, gssm_ref, wout_ref, gffn_ref, wq_ref,
               h_ref, xq_ref, qp_ref):
    aw = o_ref.shape[-1]
    z = _gelu(y_ref[...])
    z = z * jax.nn.sigmoid(_dot(z.astype(BF16), wglu_ref[...]) + bglu_ref[...])
    ys = z * lax.rsqrt(jnp.mean(z * z, axis=-1, keepdims=True) + NORM_EPS) * gssm_ref[...]
    h = (x_ref[...] + _dot(o_ref[...].astype(BF16), wout_ref[0:aw, :])
         + _dot(ys.astype(BF16), wout_ref[aw:, :]))
    h_ref[...] = h
    xq = (h * lax.rsqrt(jnp.mean(h * h, axis=-1, keepdims=True) + NORM_EPS) * gffn_ref[...]).astype(BF16)
    xq_ref[...] = xq
    qp_ref[...] = _dot(xq, wq_ref[...]).astype(BF16)


def _post(y, o, x2, w_glu_b, b_glu, g_ssm, w_out_b, g_ffn, w_query_b, tb):
    t, d = x2.shape
    aw, sw = o.shape[1], y.shape[1]
    nq = w_query_b.shape[1]
    tb = min(tb, t)
    row = lambda w: pl.BlockSpec((tb, w), lambda i: (i, 0))
    return pl.pallas_call(
        _post_body,
        grid=(t // tb,),
        in_specs=[row(sw), row(aw), row(d), _const_spec((sw, sw)), _const_spec((1, sw)), _const_spec((1, sw)),
                  _const_spec((d, d)), _const_spec((1, d)), _const_spec((d, nq))],
        out_specs=[row(d), row(d), row(nq)],
        out_shape=[jax.ShapeDtypeStruct((t, d), F32), jax.ShapeDtypeStruct((t, d), BF16),
                   jax.ShapeDtypeStruct((t, nq), BF16)],
        compiler_params=_params("parallel"),
        name="post",
    )(y, o, x2, w_glu_b, b_glu, g_ssm, w_out_b, g_ffn, w_query_b)


def _top_rows(x, k):
    nrows = x.shape[0]
    rid = lax.broadcasted_iota(jnp.int32, x.shape, 0)
    out = []
    for _ in range(k):
        m = jnp.max(x, axis=0, keepdims=True)
        first = jnp.min(jnp.where(x == m, rid, nrows), axis=0, keepdims=True)
        x = jnp.where(rid == first, -jnp.inf, x)
        out.append(m)
    return out


def _peer_select_body(qp_ref, sk_ref, s0_ref, e0_ref, s1_ref, e1_ref, tau_ref):
    nk = PEER_NKEYS
    s0 = lax.dot_general(sk_ref[0, 0], qp_ref[:, 0:nk], _NT, preferred_element_type=F32)
    s1 = lax.dot_general(sk_ref[0, 1], qp_ref[:, nk:2 * nk], _NT, preferred_element_type=F32)
    a = _top_rows(s0, PEER_TOPK)
    b = _top_rows(s1, PEER_TOPK)
    cand = [a[i] + b[j] for i in range(PEER_TOPK) for j in range(PEER_TOPK // (i + 1))]
    pad = -len(cand) % SUBLANES
    cand = jnp.concatenate(cand + [jnp.full_like(a[0], -jnp.inf)] * pad, axis=0)
    f = _top_rows(cand, PEER_TOPK)
    z = jnp.exp(f[0] - f[0])
    for fv in f[1:]:
        z = z + jnp.exp(fv - f[0])
    s0_ref[0] = s0
    s1_ref[0] = s1
    e0_ref[0] = jnp.exp(s0 - a[0])
    e1_ref[0] = jnp.exp(s1 - b[0]) / z
    tau_ref[0] = f[PEER_TOPK - 1]


def _peer_select(qp, sub_keys_b, tb):
    t = qp.shape[0]
    nk = PEER_NKEYS
    tb = min(tb, t)
    tile = pl.BlockSpec((1, nk, tb), lambda i, h: (h, 0, i))
    full = jax.ShapeDtypeStruct((PEER_HEADS, nk, t), F32)
    return pl.pallas_call(
        _peer_select_body,
        grid=(t // tb, PEER_HEADS),
        in_specs=[pl.BlockSpec((tb, 2 * nk), lambda i, h: (i, h)),
                  pl.BlockSpec((1, 2, nk, nk), lambda i, h: (h, 0, 0, 0))],
        out_specs=[tile, tile, tile, tile, pl.BlockSpec((1, 1, tb), lambda i, h: (h, 0, i))],
        out_shape=[full, full, full, full, jax.ShapeDtypeStruct((PEER_HEADS, 1, t), F32)],
        compiler_params=_params("parallel", "parallel"),
        name="peer_select",
    )(qp, sub_keys_b)


def _peer_body(xq_ref, s0_ref, e0_ref, s1_ref, e1_ref, tau_ref, u_ref, vt_ref, h_ref, y_ref,
               acc_scr, wt_scr):
    e = pl.program_id(1)
    nk = PEER_NKEYS
    per_block = u_ref.shape[0] // nk

    @pl.when(e == 0)
    def _():
        acc_scr[...] = jnp.zeros(acc_scr.shape, F32)

    hid = lax.dot_general(u_ref[...], xq_ref[...], _NT, preferred_element_type=F32)
    for il in range(per_block):
        i = e * per_block + il
        gate = jnp.zeros((nk, hid.shape[1]), F32)
        for h in range(PEER_HEADS):
            total = s0_ref[h, pl.ds(i, 1), :] + s1_ref[h]
            gate = gate + jnp.where(total >= tau_ref[h], e0_ref[h, pl.ds(i, 1), :] * e1_ref[h], 0.0)
        wt_scr[il * nk:(il + 1) * nk, :] = (gate * _gelu(hid[il * nk:(il + 1) * nk, :])).astype(BF16)
    acc_scr[...] += _dot(vt_ref[...], wt_scr[...])

    @pl.when(e == pl.num_programs(1) - 1)
    def _():
        y_ref[...] = h_ref[...] + acc_scr[...].T


def _peer(xq, sel, u_b, vt_b, h, tb, eb):
    t, d = xq.shape
    ne = u_b.shape[0]
    nk = PEER_NKEYS
    tb = min(tb, t)
    single = pl.Buffered(1)
    tile = pl.BlockSpec((PEER_HEADS, nk, tb), lambda i, e: (0, 0, i), pipeline_mode=single)
    return pl.pallas_call(
        _peer_body,
        grid=(t // tb, ne // eb),
        in_specs=[pl.BlockSpec((tb, d), lambda i, e: (i, 0), pipeline_mode=single),
                  tile, tile, tile, tile,
                  pl.BlockSpec((PEER_HEADS, 1, tb), lambda i, e: (0, 0, i), pipeline_mode=single),
                  pl.BlockSpec((eb, d), lambda i, e: (e, 0)),
                  pl.BlockSpec((d, eb), lambda i, e: (0, e)),
                  pl.BlockSpec((tb, d), lambda i, e: (i, 0), pipeline_mode=single)],
        out_specs=pl.BlockSpec((tb, d), lambda i, e: (i, 0)),
        out_shape=jax.ShapeDtypeStruct((t, d), F32),
        scratch_shapes=[pltpu.VMEM((d, tb), F32), pltpu.VMEM((eb, tb), BF16)],
        compiler_params=_params("parallel", "arbitrary"),
        name="peer",
    )(xq, *sel, u_b, vt_b, h)


def _layer(x, k_past, v_past, h0_re, h0_im, lambda_init, w):
    b, s, d = x.shape
    t = b * s
    aw = ATT_HEADS * V_DIM
    x2 = x.reshape(t, d)
    qb, k, kb, v, vb, u = _in_proj(x2, w['g_mix'], w['w_in'], w['seg'], w['gq'], w['gk'], tb=256)
    if k_past is None:
        assert b == 1
        o = _attn_prompt(qb, kb, vb, w['lam'], w['g_sub'], lambda_init, tq=512, tk=1024)
        h0_re = jnp.zeros((b, w['nstate']), F32)
        h0_im = h0_re
        ssm_tb = 512
    else:
        past = k_past.shape[1]
        r3 = lambda a: a.reshape(b, s, aw)
        o = _attn_sample(r3(qb), k_past.reshape(b, past, aw), v_past.reshape(b, past, aw), r3(kb), r3(vb),
                         w['lam'], w['g_sub'], lambda_init).reshape(t, aw)
        ssm_tb = s
    y, h_re, h_im = _ssm(u.reshape(b, s, -1), h0_re.reshape(b, -1), h0_im.reshape(b, -1),
                         w['bw'], w['cw'], w['pw'], w['d_skip'], tb=ssm_tb)
    hres, xq, qp = _post(y.reshape(t, -1), o, x2, w['w_glu'], w['b_glu'], w['g_ssm'], w['w_out'], w['g_ffn'],
                         w['w_query'], tb=256)
    sel = _peer_select(qp, w['sub_keys'], tb=512)
    out = _peer(xq, sel, w['u_table'], w['v_table_t'], hres, tb=512, eb=1024)
    ngroups = w['nstate'] // SSM_STATE
    return (out.reshape(b, s, d), k.reshape(b, s, ATT_HEADS, 2, QK_DIM), v.reshape(b, s, ATT_HEADS, V_DIM),
            h_re.reshape(b, ngroups, SSM_STATE), h_im.reshape(b, ngroups, SSM_STATE))


def _prep_weights(l, g_mix, w_in, g_q, g_k, lam_q1, lam_k1, lam_q2, lam_k2, g_sub, a_re, a_im, log_dt, b_re,
                  b_im, c_re, c_im, d_skip, w_glu, b_glu, g_ssm, w_out, g_ffn, w_query, sub_keys, u_table,
                  v_table):
    aw = ATT_HEADS * V_DIM
    row = lambda a: a[l].reshape(1, -1)
    seg_id = jnp.arange(aw) // QK_DIM
    seg = jnp.where(seg_id[:, None] == seg_id[None, :], 1.0 / QK_DIM, 0.0).astype(BF16)
    bw, cw, pw = _ssm_params(a_re[l], a_im[l], log_dt[l], b_re[l], b_im[l], c_re[l], c_im[l])
    return dict(
        g_mix=row(g_mix), w_in=w_in[l].astype(BF16), seg=seg,
        gq=jnp.tile(g_q[l], aw // QK_DIM).reshape(1, aw) * (QK_DIM ** -0.5),
        gk=jnp.tile(g_k[l], aw // QK_DIM).reshape(1, aw),
        lam=(row(lam_q1), row(lam_k1), row(lam_q2), row(lam_k2)), g_sub=row(g_sub),
        bw=bw, cw=cw, pw=pw, d_skip=row(d_skip), nstate=a_re.shape[1] * a_re.shape[2],
        w_glu=w_glu[l].astype(BF16), b_glu=row(b_glu), g_ssm=row(g_ssm), w_out=w_out[l].astype(BF16),
        g_ffn=row(g_ffn), w_query=w_query[l].astype(BF16), sub_keys=sub_keys[l].astype(BF16),
        u_table=u_table[l].astype(BF16), v_table_t=v_table[l].T.astype(BF16))


def kernel(x_prompt, x_sample, cache_k, cache_v, state_ssm_re, state_ssm_im, g_mix, w_in, g_q, g_k, lam_q1, lam_k1, lam_q2, lam_k2, g_sub, a_re, a_im, log_dt, b_re, b_im, c_re, c_im, d_skip, w_glu, b_glu, g_ssm, w_out, g_ffn, w_query, sub_keys, u_table, v_table):
    depth = w_in.shape[0]
    yp, ys = x_prompt, x_sample
    outs = [[] for _ in range(8)]
    for l in range(depth):
        lambda_init = 0.8 - 0.6 * math.exp(-0.3 * l)
        w = _prep_weights(l, g_mix, w_in, g_q, g_k, lam_q1, lam_k1, lam_q2, lam_k2, g_sub, a_re, a_im, log_dt,
                          b_re, b_im, c_re, c_im, d_skip, w_glu, b_glu, g_ssm, w_out, g_ffn, w_query, sub_keys,
                          u_table, v_table)
        yp, kp, vp, hrp, hip = _layer(yp, None, None, None, None, lambda_init, w)
        ys, kk, vv, hrs, his = _layer(ys, cache_k[l], cache_v[l], state_ssm_re[l], state_ssm_im[l],
                                      lambda_init, w)
        for acc, val in zip(outs, (kp, vp, hrp, hip, kk, vv, hrs, his)):
            acc.append(val)
    return (yp, ys) + tuple(jnp.stack(o) for o in outs)
```

```python
import functools
import math

import jax
import jax.numpy as jnp
from jax import lax
from jax.experimental import pallas as pl
from jax.experimental.pallas import tpu as pltpu

F32 = jnp.float32
BF16 = jnp.bfloat16

CHUNK = 64
ATT_HEADS = 8
QK_DIM = 64
V_DIM = 2 * QK_DIM
SSM_GROUP = 16
SSM_STATE = 64
PEER_HEADS = 8
PEER_NKEYS = 128
PEER_TOPK = 16
NORM_EPS = 1e-6

LANES = 128
SUBLANES = 8
MXU_DIM = 256
GATE_ROWS = 32
SSM_SUPER = LANES // SSM_GROUP
SSM_SLAB = SSM_SUPER * SSM_STATE
VMEM_LIMIT = 56 * 1024 * 1024
SQRT_HALF = math.sqrt(0.5)
LOG2_E = math.log2(math.e)
BF16_SLACK = 1.0 + 2.0 ** -6
MAX_UNSHIFTED_LOG2 = 60.0

_NT = (((1,), (1,)), ((), ()))


def _params(*sem):
    return pltpu.CompilerParams(dimension_semantics=sem, vmem_limit_bytes=VMEM_LIMIT)


def _const_spec(shape):
    nd = len(shape)
    return pl.BlockSpec(shape, lambda *_: (0,) * nd, pipeline_mode=pl.Buffered(1))


def _gelu(x):
    return 0.5 * x * (1.0 + lax.erf(x * SQRT_HALF))


def _dot(a, b):
    return jnp.dot(a, b, preferred_element_type=F32)


def _in_proj_body(x_ref, gmix_ref, w_ref, seg_ref, gq_ref, gk_ref,
                  qb_ref, k_ref, kb_ref, v_ref, vb_ref, u_ref):
    aw = k_ref.shape[-1]
    x = x_ref[...]
    ms = jnp.mean(x * x, axis=-1, keepdims=True)
    xn = (x * lax.rsqrt(ms + NORM_EPS) * gmix_ref[...]).astype(BF16)

    def seg_rms(y, g):
        sq = y * y
        hi = sq.astype(BF16)
        lo = (sq - hi.astype(F32)).astype(BF16)
        mean = _dot(hi, seg_ref[...]) + _dot(lo, seg_ref[...])
        return y * lax.rsqrt(mean + NORM_EPS) * g

    q = seg_rms(_dot(xn, w_ref[:, 0:aw]), gq_ref[...])
    qb_ref[...] = q.astype(BF16)
    k = seg_rms(_dot(xn, w_ref[:, aw:2 * aw]), gk_ref[...])
    k_ref[...] = k
    kb_ref[...] = k.astype(BF16)
    v = _dot(xn, w_ref[:, 2 * aw:3 * aw])
    v_ref[...] = v
    vb = v.astype(BF16)
    ones = jnp.ones((vb.shape[0], V_DIM), BF16)
    for h in range(aw // V_DIM):
        vb_ref[:, 2 * h * V_DIM:(2 * h + 1) * V_DIM] = vb[:, h * V_DIM:(h + 1) * V_DIM]
        vb_ref[:, (2 * h + 1) * V_DIM:(2 * h + 2) * V_DIM] = ones
    u_ref[...] = _dot(xn, w_ref[:, 3 * aw:])


def _in_proj(x2, g_mix, w_in_b, seg, gq_t, gk_t, tb):
    t, d = x2.shape
    n = w_in_b.shape[1]
    aw = seg.shape[0]
    sw = n - 3 * aw
    tb = min(tb, t)
    row = lambda w: pl.BlockSpec((tb, w), lambda i: (i, 0))
    return pl.pallas_call(
        _in_proj_body,
        grid=(t // tb,),
        in_specs=[row(d), _const_spec((1, d)), _const_spec((d, n)), _const_spec((aw, aw)),
                  _const_spec((1, aw)), _const_spec((1, aw))],
        out_specs=[row(aw), row(aw), row(aw), row(aw), row(2 * aw), row(sw)],
        out_shape=[jax.ShapeDtypeStruct((t, aw), BF16), jax.ShapeDtypeStruct((t, aw), F32),
                   jax.ShapeDtypeStruct((t, aw), BF16), jax.ShapeDtypeStruct((t, aw), F32),
                   jax.ShapeDtypeStruct((t, 2 * aw), BF16), jax.ShapeDtypeStruct((t, sw), F32)],
        compiler_params=_params("parallel"),
        name="in_proj",
    )(x2, g_mix, w_in_b, seg, gq_t, gk_t)


def _lambda(lq1_ref, lk1_ref, lq2_ref, lk2_ref, lambda_init):
    s1 = jnp.sum(lq1_ref[...] * lk1_ref[...], axis=-1, keepdims=True)
    s2 = jnp.sum(lq2_ref[...] * lk2_ref[...], axis=-1, keepdims=True)
    return jnp.exp(s1) - jnp.exp(s2) + lambda_init


def _stack_maps(q):
    lane = lax.broadcasted_iota(jnp.int32, q.shape, 1)
    zero = jnp.zeros_like(q)
    return jnp.concatenate([jnp.where(lane < QK_DIM, q, zero), jnp.where(lane >= QK_DIM, q, zero)], axis=0)


def _sub_norm(o2, l, lam, gsub, nq, lambda_init):
    o2 = o2 / l
    o = o2[:nq] - lam * o2[nq:]
    ms = jnp.mean(o * o, axis=-1, keepdims=True)
    return (o * lax.rsqrt(ms + NORM_EPS) * gsub) * (1.0 - lambda_init)


def _attn_prompt_body(bounded_ref, lq1_ref, lk1_ref, lq2_ref, lk2_ref, gsub_ref, q_ref, k_ref, v_ref, o_ref,
                      q2_scr, m_scr, acc_scr, *, tq, tk, rt, lambda_init):
    i = pl.program_id(1)
    q2_scr[...] = _stack_maps(q_ref[...])
    acc_scr[...] = jnp.zeros(acc_scr.shape, F32)
    jd = (i * tq) // tk

    def scores(j, r0, masked):
        keys = pl.ds(pl.multiple_of(j * tk, tk), tk)
        s = lax.dot_general(q2_scr[r0:r0 + rt, :], k_ref[keys, :], _NT, preferred_element_type=F32)
        if masked:
            row = lax.broadcasted_iota(jnp.int32, s.shape, 0)
            col = lax.broadcasted_iota(jnp.int32, s.shape, 1)
            q_chunk = (i * tq + (r0 % tq) + row) // CHUNK
            k_chunk = (j * tk + col) // CHUNK
            s = jnp.where(k_chunk <= q_chunk, s, -jnp.inf)
        return s, keys

    def unshifted_step(j, masked):
        for r0 in range(0, 2 * tq, rt):
            s, keys = scores(j, r0, masked)
            acc_scr[r0:r0 + rt, :] += _dot(jnp.exp2(s).astype(BF16), v_ref[keys, :])

    def online_step(j, masked):
        for r0 in range(0, 2 * tq, rt):
            rows = slice(r0, r0 + rt)
            s, keys = scores(j, r0, masked)
            m_prev = m_scr[rows, :]
            m_new = jnp.maximum(m_prev, jnp.max(s, axis=1, keepdims=True))
            p = jnp.exp2(s - m_new)
            acc_scr[rows, :] = jnp.exp2(m_prev - m_new) * acc_scr[rows, :] + _dot(p.astype(BF16), v_ref[keys, :])
            m_scr[rows, :] = m_new

    def sweep(step):
        def full_step(j, c):
            step(j, False)
            return c
        lax.fori_loop(0, jd, full_step, 0)
        step(jd, True)

    @pl.when(bounded_ref[0] == 1)
    def _():
        sweep(unshifted_step)

    @pl.when(bounded_ref[0] != 1)
    def _():
        m_scr[...] = jnp.full(m_scr.shape, -jnp.inf, F32)
        sweep(online_step)

    lam = _lambda(lq1_ref, lk1_ref, lq2_ref, lk2_ref, lambda_init)
    acc = acc_scr[...]
    o_ref[...] = _sub_norm(acc[:, :V_DIM], acc[:, V_DIM:V_DIM + 1], lam, gsub_ref[...], tq, lambda_init)


def _attn_prompt(bounded, qb, kb, vb1, lam_parts, g_sub, lambda_init, tq, tk, rt):
    s, aw = qb.shape
    tk = min(tk, s)
    tq = min(tq, tk)
    rt = min(rt, tq)
    assert s % tk == 0 and tk % tq == 0 and tq % rt == 0 and rt % CHUNK == 0
    lam_specs = [_const_spec((1, QK_DIM))] * 4
    return pl.pallas_call(
        functools.partial(_attn_prompt_body, tq=tq, tk=tk, rt=rt, lambda_init=lambda_init),
        grid=(ATT_HEADS, s // tq),
        in_specs=[pl.BlockSpec(memory_space=pltpu.SMEM)] + lam_specs + [
            _const_spec((1, V_DIM)), pl.BlockSpec((tq, V_DIM), lambda h, i: (i, h)),
            pl.BlockSpec((s, V_DIM), lambda h, i: (0, h)), pl.BlockSpec((s, 2 * V_DIM), lambda h, i: (0, h))],
        out_specs=pl.BlockSpec((tq, V_DIM), lambda h, i: (i, h)),
        out_shape=jax.ShapeDtypeStruct((s, aw), F32),
        scratch_shapes=[pltpu.VMEM((2 * tq, V_DIM), BF16), pltpu.VMEM((2 * tq, 1), F32),
                        pltpu.VMEM((2 * tq, 2 * V_DIM), F32)],
        compiler_params=_params("parallel", "arbitrary"),
        name="attn_prompt",
    )(bounded, *lam_parts, g_sub, qb, kb, vb1)


def _attn_sample_body(lq1_ref, lk1_ref, lq2_ref, lk2_ref, gsub_ref, q_ref, kp_ref, vp_ref, kn_ref, vn_ref,
                      o_ref, *, past, lambda_init):
    nq = q_ref.shape[1]
    q2 = _stack_maps(q_ref[0])
    s_past = lax.dot_general(q2, kp_ref[0].astype(BF16), _NT, preferred_element_type=F32)
    s_new = lax.dot_general(q2, kn_ref[0], _NT, preferred_element_type=F32)
    row = lax.broadcasted_iota(jnp.int32, s_new.shape, 0)
    col = lax.broadcasted_iota(jnp.int32, s_new.shape, 1)
    q_chunk = (past + jnp.where(row >= nq, row - nq, row)) // CHUNK
    s_new = jnp.where((past + col) // CHUNK <= q_chunk, s_new, -jnp.inf)
    m = jnp.maximum(jnp.max(s_past, axis=1, keepdims=True), jnp.max(s_new, axis=1, keepdims=True))
    p_past = jnp.exp2(s_past - m)
    p_new = jnp.exp2(s_new - m)
    l = jnp.sum(p_past, axis=1, keepdims=True) + jnp.sum(p_new, axis=1, keepdims=True)
    acc = _dot(p_past.astype(BF16), vp_ref[0].astype(BF16)) + _dot(p_new.astype(BF16), vn_ref[0])
    lam = _lambda(lq1_ref, lk1_ref, lq2_ref, lk2_ref, lambda_init)
    o_ref[0] = _sub_norm(acc, l, lam, gsub_ref[...], nq, lambda_init)


def _attn_sample(qb3, kp3, vp3, kb3, vb3, lam_parts, g_sub, lambda_init):
    b, s, aw = qb3.shape
    past = kp3.shape[1]
    assert past % CHUNK == 0
    lam_specs = [_const_spec((1, QK_DIM))] * 4
    new = pl.BlockSpec((1, s, V_DIM), lambda bi, h: (bi, 0, h))
    new_v = pl.BlockSpec((1, s, V_DIM), lambda bi, h: (bi, 0, 2 * h))
    old = pl.BlockSpec((1, past, V_DIM), lambda bi, h: (bi, 0, h))
    return pl.pallas_call(
        functools.partial(_attn_sample_body, past=past, lambda_init=lambda_init),
        grid=(b, ATT_HEADS),
        in_specs=lam_specs + [_const_spec((1, V_DIM)), new, old, old, new, new_v],
        out_specs=new,
        out_shape=jax.ShapeDtypeStruct((b, s, aw), F32),
        compiler_params=_params("parallel", "arbitrary"),
        name="attn_sample",
    )(*lam_parts, g_sub, qb3, kp3, vp3, kb3, vb3)


def _ssm_param_body(are_ref, aim_ref, ldt_ref, bre_ref, bim_ref, abr_ref, abi_ref, bbr_ref, bbi_ref):
    ar, ai = are_ref[...], aim_ref[...]
    dt = jnp.exp(ldt_ref[...])
    mag = jnp.exp(ar * dt)
    abr, abi = mag * jnp.cos(ai * dt), mag * jnp.sin(ai * dt)
    den = ar * ar + ai * ai
    nr, ni = abr - 1.0, abi
    fr, fi = (nr * ar + ni * ai) / den, (ni * ar - nr * ai) / den
    br, bi = bre_ref[...], bim_ref[...]
    abr_ref[...] = abr
    abi_ref[...] = abi
    bbr_ref[...] = fr * br - fi * bi
    bbi_ref[...] = fr * bi + fi * br


def _ssm_params(a_re, a_im, log_dt, b_re, b_im, c_re, c_im):
    g, p = a_re.shape
    c = b_re.shape[-1]
    n = g * p
    col = lambda a: a.reshape(n, 1)
    abr, abi, bbr, bbi = pl.pallas_call(
        _ssm_param_body,
        out_shape=[jax.ShapeDtypeStruct((n, 1), F32)] * 2 + [jax.ShapeDtypeStruct((n, c), F32)] * 2,
        name="ssm_params",
    )(col(a_re), col(a_im), col(jnp.repeat(log_dt, p)), b_re.reshape(n, c), b_im.reshape(n, c))
    ns = g // SSM_SUPER
    eye = jnp.eye(SSM_SUPER, dtype=F32)

    def in_blocks(bb):
        bb = bb.reshape(ns, SSM_SUPER, p, c)
        return jnp.einsum('sgpc,gh->sgchp', bb, eye).reshape(ns, SSM_SUPER * c, SSM_SLAB)

    def out_blocks(cc):
        cc = cc.reshape(ns, SSM_SUPER, c, p)
        return jnp.einsum('sgcp,gh->sgphc', cc, eye).reshape(ns, SSM_SLAB, SSM_SUPER * c)

    bw = jnp.concatenate([in_blocks(bbr), in_blocks(bbi)], axis=2).astype(BF16)
    cw = jnp.concatenate([out_blocks(c_re), -out_blocks(c_im)], axis=1).astype(BF16)
    pr, pi = [abr.reshape(ns, 1, SSM_SLAB)], [abi.reshape(ns, 1, SSM_SLAB)]
    for _ in range(SUBLANES - 1):
        pr, pi = (pr + [pr[-1] * pr[0] - pi[-1] * pi[0]], pi + [pr[-1] * pi[0] + pi[-1] * pr[0]])
    rows = jnp.arange(SUBLANES).reshape(1, SUBLANES, 1)
    tabs = []
    for k in (1, 2, 4):
        tabs += [jnp.where(rows >= k, pr[k - 1], 0.0), jnp.where(rows >= k, pi[k - 1], 0.0)]
    tabs += [jnp.concatenate(pr, axis=1), jnp.concatenate(pi, axis=1)]
    return bw, cw, jnp.stack(tabs, axis=1)


def _ssm_body(u_ref, bw_ref, cw_ref, pw_ref, dsk_ref, h0r_ref, h0i_ref, y_ref, hr_ref, hi_ref,
              h_scr, c_scr):
    tb = pl.program_id(2)
    nrow = u_ref.shape[1]
    ns = SSM_SLAB

    @pl.when(tb == 0)
    def _():
        c_scr[0:1, :] = h0r_ref[0]
        c_scr[1:2, :] = h0i_ref[0]

    u = u_ref[0]
    h_scr[...] = _dot(u.astype(BF16), bw_ref[0])
    t1r, t1i, t2r, t2i, t4r, t4i, tcr, tci = (pw_ref[0, n] for n in range(8))

    def body(r, carry):
        cr, ci = carry
        rows = pl.ds(pl.multiple_of(r * SUBLANES, SUBLANES), SUBLANES)
        xr, xi = h_scr[rows, 0:ns], h_scr[rows, ns:2 * ns]
        for k, pr, pi in ((1, t1r, t1i), (2, t2r, t2i), (4, t4r, t4i)):
            sr, si = pltpu.roll(xr, k, 0), pltpu.roll(xi, k, 0)
            xr, xi = xr + pr * sr - pi * si, xi + pr * si + pi * sr
        xr, xi = xr + tcr * cr - tci * ci, xi + tcr * ci + tci * cr
        h_scr[rows, 0:ns] = xr
        h_scr[rows, ns:2 * ns] = xi
        return xr[SUBLANES - 1:SUBLANES], xi[SUBLANES - 1:SUBLANES]

    cr, ci = lax.fori_loop(0, nrow // SUBLANES, body, (c_scr[0:1, :], c_scr[1:2, :]))
    c_scr[0:1, :] = cr
    c_scr[1:2, :] = ci
    y_ref[0] = _dot(h_scr[...].astype(BF16), cw_ref[0]) + dsk_ref[...] * u

    @pl.when(tb == pl.num_programs(2) - 1)
    def _():
        hr_ref[0] = cr
        hi_ref[0] = ci


def _ssm(u3, h0r, h0i, bw, cw, pw, d_skip, tb):
    b, s, sw = u3.shape
    ns = sw // LANES
    tb = min(tb, s)
    assert s % tb == 0 and tb % SUBLANES == 0
    nstate = ns * SSM_SLAB
    slab = lambda r, c: pl.BlockSpec((1, r, c), lambda bi, g, t: (g, 0, 0))
    state = pl.BlockSpec((1, 1, SSM_SLAB), lambda bi, g, t: (bi, 0, g))
    seq = pl.BlockSpec((1, tb, LANES), lambda bi, g, t: (bi, t, g))
    return pl.pallas_call(
        _ssm_body,
        grid=(b, ns, s // tb),
        in_specs=[seq, slab(LANES, 2 * SSM_SLAB), slab(2 * SSM_SLAB, LANES),
                  pl.BlockSpec((1, 8, SUBLANES, SSM_SLAB), lambda bi, g, t: (g, 0, 0, 0)),
                  pl.BlockSpec((1, LANES), lambda bi, g, t: (0, g)), state, state],
        out_specs=[seq, state, state],
        out_shape=[jax.ShapeDtypeStruct((b, s, sw), F32), jax.ShapeDtypeStruct((b, 1, nstate), F32),
                   jax.ShapeDtypeStruct((b, 1, nstate), F32)],
        scratch_shapes=[pltpu.VMEM((tb, 2 * SSM_SLAB), F32), pltpu.VMEM((SUBLANES, SSM_SLAB), F32)],
        compiler_params=_params("parallel", "parallel", "arbitrary"),
        name="ssm",
    )(u3, bw, cw, pw, d_skip, h0r.reshape(b, 1, nstate), h0i.reshape(b, 1, nstate))


def _post_body(y_ref, o_ref, x_ref, wglu_ref, bglu_ref, gssm_ref, wout_ref, gffn_ref, wq_ref,
               h_ref, xq_ref, qp_ref):
    aw = o_ref.shape[-1]
    z = _gelu(y_ref[...])
    z = z * jax.nn.sigmoid(_dot(z.astype(BF16), wglu_ref[...]) + bglu_ref[...])
    ys = z * lax.rsqrt(jnp.mean(z * z, axis=-1, keepdims=True) + NORM_EPS) * gssm_ref[...]
    h = (x_ref[...] + _dot(o_ref[...].astype(BF16), wout_ref[0:aw, :])
         + _dot(ys.astype(BF16), wout_ref[aw:, :]))
    h_ref[...] = h
    xq = (h * lax.rsqrt(jnp.mean(h * h, axis=-1, keepdims=True) + NORM_EPS) * gffn_ref[...]).astype(BF16)
    xq_ref[...] = xq
    qp_ref[...] = _dot(xq, wq_ref[...]).astype(BF16)


def _post(y, o, x2, w_glu_b, b_glu, g_ssm, w_out_b, g_ffn, w_query_b, tb):
    t, d = x2.shape
    aw, sw = o.shape[1], y.shape[1]
    nq = w_query_b.shape[1]
    tb = min(tb, t)
    row = lambda w: pl.BlockSpec((tb, w), lambda i: (i, 0))
    return pl.pallas_call(
        _post_body,
        grid=(t // tb,),
        in_specs=[row(sw), row(aw), row(d), _const_spec((sw, sw)), _const_spec((1, sw)), _const_spec((1, sw)),
                  _const_spec((d, d)), _const_spec((1, d)), _const_spec((d, nq))],
        out_specs=[row(d), row(d), row(nq)],
        out_shape=[jax.ShapeDtypeStruct((t, d), F32), jax.ShapeDtypeStruct((t, d), BF16),
                   jax.ShapeDtypeStruct((t, nq), BF16)],
        compiler_params=_params("parallel"),
        name="post",
    )(y, o, x2, w_glu_b, b_glu, g_ssm, w_out_b, g_ffn, w_query_b)


def _top_rows(x, k):
    nrows = x.shape[0]
    rid = lax.broadcasted_iota(jnp.int32, x.shape, 0)
    vals, idx = [], []
    for _ in range(k):
        m = jnp.max(x, axis=0, keepdims=True)
        first = jnp.min(jnp.where(x == m, rid, nrows), axis=0, keepdims=True)
        x = jnp.where(rid == first, -jnp.inf, x)
        vals.append(m)
        idx.append(first)
    return vals, idx


def _peer_select_body(qp_ref, sk_ref, e0_ref, d_ref, s1_ref, e1_ref):
    nk = PEER_NKEYS
    s0 = lax.dot_general(sk_ref[0, 0], qp_ref[:, 0:nk], _NT, preferred_element_type=F32)
    s1 = lax.dot_general(sk_ref[0, 1], qp_ref[:, nk:2 * nk], _NT, preferred_element_type=F32)
    a, a_idx = _top_rows(s0, PEER_TOPK)
    b, _ = _top_rows(s1, PEER_TOPK)
    width = [PEER_TOPK // (r + 1) for r in range(PEER_TOPK)]
    cand = [[a[r] + b[c] for c in range(width[r])] for r in range(PEER_TOPK)]
    flat = [x for row in cand for x in row]
    pad = -len(flat) % SUBLANES
    f, _ = _top_rows(jnp.concatenate(flat + [jnp.full_like(a[0], -jnp.inf)] * pad, axis=0), PEER_TOPK)
    tau = f[PEER_TOPK - 1]
    z = jnp.exp(f[0] - f[0])
    for fv in f[1:]:
        z = z + jnp.exp(fv - f[0])
    rid = lax.broadcasted_iota(jnp.int32, s0.shape, 0)
    d = jnp.full(s0.shape, jnp.inf, F32)
    for r in range(PEER_TOPK):
        d_r = jnp.full_like(tau, jnp.inf)
        for c in range(width[r]):
            d_r = jnp.where(cand[r][c] >= tau, b[c], d_r)
        d = jnp.where(rid == a_idx[r], d_r, d)
    d_ref[0] = d
    s1_ref[0] = s1
    e0_ref[0] = jnp.exp(s0 - a[0])
    e1_ref[0] = jnp.exp(s1 - b[0]) / z


def _peer_select(qp, sub_keys_b, tb):
    t = qp.shape[0]
    nk = PEER_NKEYS
    tb = min(tb, t)
    tile = pl.BlockSpec((1, nk, tb), lambda i, h: (h, 0, i))
    full = jax.ShapeDtypeStruct((PEER_HEADS, nk, t), F32)
    return pl.pallas_call(
        _peer_select_body,
        grid=(t // tb, PEER_HEADS),
        in_specs=[pl.BlockSpec((tb, 2 * nk), lambda i, h: (i, h)),
                  pl.BlockSpec((1, 2, nk, nk), lambda i, h: (h, 0, 0, 0))],
        out_specs=[tile, tile, tile, tile],
        out_shape=[full, full, full, full],
        compiler_params=_params("parallel", "parallel"),
        name="peer_select",
    )(qp, sub_keys_b)


def _peer_body(xq_ref, e0_ref, d_ref, s1_ref, e1_ref, u_ref, vt_ref, h_ref, y_ref,
               acc_scr, wt_scr, *, nblocks, pieces):
    e = pl.program_id(1)
    nk = PEER_NKEYS
    eb, d = u_ref.shape
    tb = xq_ref.shape[0]
    per_block = eb // nk
    rows_a, rows_c = eb // pieces, d // pieces

    @pl.when(e == 0)
    def _():
        acc_scr[...] = jnp.zeros(acc_scr.shape, F32)
        wt_scr[...] = jnp.zeros(wt_scr.shape, BF16)

    blk = jnp.minimum(e, nblocks - 1)
    first_keys = pl.ds(pl.multiple_of(blk * per_block, per_block), per_block)

    def step(cur, prev):
        d_rows = [d_ref[h, first_keys, :] for h in range(PEER_HEADS)]
        e0_rows = [e0_ref[h, first_keys, :] for h in range(PEER_HEADS)]
        tw = min(MXU_DIM, tb)
        for t0 in range(0, tb, tw):
            toks = slice(t0, t0 + tw)
            for p in range(pieces):
                hid = lax.dot_general(u_ref[p * rows_a:(p + 1) * rows_a, :], xq_ref[toks, :], _NT,
                                      preferred_element_type=F32)
                slab = slice(p * rows_c, (p + 1) * rows_c)
                acc_scr[slab, toks] += _dot(vt_ref[slab, :], wt_scr[prev, :, toks])
                for il in range(rows_a // nk):
                    i = p * (rows_a // nk) + il
                    for c in range(0, tw, LANES):
                        lanes = slice(t0 + c, t0 + c + LANES)
                        for j0 in range(0, nk, GATE_ROWS):
                            keys = slice(j0, j0 + GATE_ROWS)
                            gate = jnp.zeros((GATE_ROWS, LANES), F32)
                            for h in range(PEER_HEADS):
                                sel = s1_ref[h, keys, lanes] >= d_rows[h][i:i + 1, lanes]
                                gate = gate + jnp.where(sel, e0_rows[h][i:i + 1, lanes] * e1_ref[h, keys, lanes], 0.0)
                            act = _gelu(hid[il * nk + j0:il * nk + j0 + GATE_ROWS, c:c + LANES])
                            rows = slice(p * rows_a + il * nk + j0, p * rows_a + il * nk + j0 + GATE_ROWS)
                            wt_scr[cur, rows, lanes] = (gate * act).astype(BF16)

    for parity in (0, 1):
        pl.when(e % 2 == parity)(functools.partial(step, parity, 1 - parity))

    @pl.when(e == pl.num_programs(1) - 1)
    def _():
        y_ref[...] = h_ref[...] + acc_scr[...].T


def _peer(xq, sel, u_b, vt_b, h, tb, eb):
    t, d = xq.shape
    ne = u_b.shape[0]
    nk = PEER_NKEYS
    tb = min(tb, t)
    nblocks = ne // eb
    single = pl.Buffered(1)
    tile = pl.BlockSpec((PEER_HEADS, nk, tb), lambda i, e: (0, 0, i), pipeline_mode=single)
    return pl.pallas_call(
        functools.partial(_peer_body, nblocks=nblocks, pieces=4),
        grid=(t // tb, nblocks + 1),
        in_specs=[pl.BlockSpec((tb, d), lambda i, e: (i, 0), pipeline_mode=single),
                  tile, tile, tile, tile,
                  pl.BlockSpec((eb, d), lambda i, e: (jnp.minimum(e, nblocks - 1), 0)),
                  pl.BlockSpec((d, eb), lambda i, e: (0, jnp.maximum(e - 1, 0))),
                  pl.BlockSpec((tb, d), lambda i, e: (i, 0), pipeline_mode=single)],
        out_specs=pl.BlockSpec((tb, d), lambda i, e: (i, 0)),
        out_shape=jax.ShapeDtypeStruct((t, d), F32),
        scratch_shapes=[pltpu.VMEM((d, tb), F32), pltpu.VMEM((2, eb, tb), BF16)],
        compiler_params=_params("parallel", "arbitrary"),
        name="peer",
    )(xq, *sel, u_b, vt_b, h)


def _layer(x, k_past, v_past, h0_re, h0_im, lambda_init, w):
    b, s, d = x.shape
    t = b * s
    aw = ATT_HEADS * V_DIM
    x2 = x.reshape(t, d)
    qb, k, kb, v, vb, u = _in_proj(x2, w['g_mix'], w['w_in'], w['seg'], w['gq'], w['gk'], tb=256)
    if k_past is None:
        assert b == 1
        o = _attn_prompt(w['bounded'], qb, kb, vb, w['lam'], w['g_sub'], lambda_init, tq=512, tk=1024, rt=256)
        h0_re = jnp.zeros((b, w['nstate']), F32)
        h0_im = h0_re
        ssm_tb = 512
    else:
        past = k_past.shape[1]
        r3 = lambda a: a.reshape(b, s, aw)
        o = _attn_sample(r3(qb), k_past.reshape(b, past, aw), v_past.reshape(b, past, aw), r3(kb),
                         vb.reshape(b, s, 2 * aw), w['lam'], w['g_sub'], lambda_init).reshape(t, aw)
        ssm_tb = s
    y, h_re, h_im = _ssm(u.reshape(b, s, -1), h0_re.reshape(b, -1), h0_im.reshape(b, -1),
                         w['bw'], w['cw'], w['pw'], w['d_skip'], tb=ssm_tb)
    hres, xq, qp = _post(y.reshape(t, -1), o, x2, w['w_glu'], w['b_glu'], w['g_ssm'], w['w_out'], w['g_ffn'],
                         w['w_query'], tb=256)
    sel = _peer_select(qp, w['sub_keys'], tb=512)
    out = _peer(xq, sel, w['u_table'], w['v_table_t'], hres, tb=512, eb=1024)
    ngroups = w['nstate'] // SSM_STATE
    return (out.reshape(b, s, d), k.reshape(b, s, ATT_HEADS, 2, QK_DIM), v.reshape(b, s, ATT_HEADS, V_DIM),
            h_re.reshape(b, ngroups, SSM_STATE), h_im.reshape(b, ngroups, SSM_STATE))


def _prep_weights(l, g_mix, w_in, g_q, g_k, lam_q1, lam_k1, lam_q2, lam_k2, g_sub, a_re, a_im, log_dt, b_re,
                  b_im, c_re, c_im, d_skip, w_glu, b_glu, g_ssm, w_out, g_ffn, w_query, sub_keys, u_table,
                  v_table):
    aw = ATT_HEADS * V_DIM
    row = lambda a: a[l].reshape(1, -1)
    seg_id = jnp.arange(aw) // QK_DIM
    seg = jnp.where(seg_id[:, None] == seg_id[None, :], 1.0 / QK_DIM, 0.0).astype(BF16)
    bw, cw, pw = _ssm_params(a_re[l], a_im[l], log_dt[l], b_re[l], b_im[l], c_re[l], c_im[l])
    q_scale = QK_DIM ** -0.5 * LOG2_E
    score_bound = QK_DIM * q_scale * jnp.max(jnp.abs(g_q[l])) * jnp.max(jnp.abs(g_k[l])) * BF16_SLACK
    return dict(
        bounded=(score_bound <= MAX_UNSHIFTED_LOG2).astype(jnp.int32).reshape(1),
        g_mix=row(g_mix), w_in=w_in[l].astype(BF16), seg=seg,
        gq=jnp.tile(g_q[l], aw // QK_DIM).reshape(1, aw) * q_scale,
        gk=jnp.tile(g_k[l], aw // QK_DIM).reshape(1, aw),
        lam=(row(lam_q1), row(lam_k1), row(lam_q2), row(lam_k2)), g_sub=row(g_sub),
        bw=bw, cw=cw, pw=pw, d_skip=row(d_skip), nstate=a_re.shape[1] * a_re.shape[2],
        w_glu=w_glu[l].astype(BF16), b_glu=row(b_glu), g_ssm=row(g_ssm), w_out=w_out[l].astype(BF16),
        g_ffn=row(g_ffn), w_query=w_query[l].astype(BF16), sub_keys=sub_keys[l].astype(BF16),
        u_table=u_table[l].astype(BF16), v_table_t=v_table[l].T.astype(BF16))


def kernel(x_prompt, x_sample, cache_k, cache_v, state_ssm_re, state_ssm_im, g_mix, w_in, g_q, g_k, lam_q1, lam_k1, lam_q2, lam_k2, g_sub, a_re, a_im, log_dt, b_re, b_im, c_re, c_im, d_skip, w_glu, b_glu, g_ssm, w_out, g_ffn, w_query, sub_keys, u_table, v_table):
    depth = w_in.shape[0]
    yp, ys = x_prompt, x_sample
    outs = [[] for _ in range(8)]
    for l in range(depth):
        lambda_init = 0.8 - 0.6 * math.exp(-0.3 * l)
        w = _prep_weights(l, g_mix, w_in, g_q, g_k, lam_q1, lam_k1, lam_q2, lam_k2, g_sub, a_re, a_im, log_dt,
                          b_re, b_im, c_re, c_im, d_skip, w_glu, b_glu, g_ssm, w_out, g_ffn, w_query, sub_keys,
                          u_table, v_table)
        yp, kp, vp, hrp, hip = _layer(yp, None, None, None, None, lambda_init, w)
        ys, kk, vv, hrs, his = _layer(ys, cache_k[l], cache_v[l], state_ssm_re[l], state_ssm_im[l],
                                      lambda_init, w)
        for acc, val in zip(outs, (kp, vp, hrp, hip, kk, vv, hrs, his)):
            acc.append(val)
    return (yp, ys) + tuple(jnp.stack(o) for o in outs)
```

```python
import functools
import math

import jax
import jax.numpy as jnp
from jax import lax
from jax.experimental import pallas as pl
from jax.experimental.pallas import tpu as pltpu

F32 = jnp.float32
BF16 = jnp.bfloat16

CHUNK = 64
ATT_HEADS = 8
QK_DIM = 64
V_DIM = 2 * QK_DIM
SSM_GROUP = 16
SSM_STATE = 64
PEER_HEADS = 8
PEER_NKEYS = 128
PEER_TOPK = 16
NORM_EPS = 1e-6

LANES = 128
SUBLANES = 8
MXU_DIM = 256
GATE_ROWS = 32
SSM_SUPER = LANES // SSM_GROUP
SSM_SLAB = SSM_SUPER * SSM_STATE
VMEM_LIMIT = 56 * 1024 * 1024
SQRT_HALF = math.sqrt(0.5)
LOG2_E = math.log2(math.e)
BF16_SLACK = 1.0 + 2.0 ** -6
MAX_UNSHIFTED_LOG2 = 60.0

_NT = (((1,), (1,)), ((), ()))


def _params(*sem):
    return pltpu.CompilerParams(dimension_semantics=sem, vmem_limit_bytes=VMEM_LIMIT)


def _const_spec(shape):
    nd = len(shape)
    return pl.BlockSpec(shape, lambda *_: (0,) * nd, pipeline_mode=pl.Buffered(1))


def _gelu(x):
    return 0.5 * x * (1.0 + lax.erf(x * SQRT_HALF))


def _dot(a, b):
    return jnp.dot(a, b, preferred_element_type=F32)


def _in_proj_body(x_ref, gmix_ref, w_ref, seg_ref, gq_ref, gk_ref,
                  qb_ref, k_ref, kb_ref, v_ref, vb_ref, u_ref):
    aw = k_ref.shape[-1]
    x = x_ref[...]
    ms = jnp.mean(x * x, axis=-1, keepdims=True)
    xn = (x * lax.rsqrt(ms + NORM_EPS) * gmix_ref[...]).astype(BF16)

    def seg_rms(y, g):
        sq = y * y
        hi = sq.astype(BF16)
        lo = (sq - hi.astype(F32)).astype(BF16)
        mean = _dot(hi, seg_ref[...]) + _dot(lo, seg_ref[...])
        return y * lax.rsqrt(mean + NORM_EPS) * g

    q = seg_rms(_dot(xn, w_ref[:, 0:aw]), gq_ref[...])
    qb_ref[...] = q.astype(BF16)
    k = seg_rms(_dot(xn, w_ref[:, aw:2 * aw]), gk_ref[...])
    k_ref[...] = k
    kb_ref[...] = k.astype(BF16)
    v = _dot(xn, w_ref[:, 2 * aw:3 * aw])
    v_ref[...] = v
    vb = v.astype(BF16)
    ones = jnp.ones((vb.shape[0], V_DIM), BF16)
    for h in range(aw // V_DIM):
        vb_ref[:, 2 * h * V_DIM:(2 * h + 1) * V_DIM] = vb[:, h * V_DIM:(h + 1) * V_DIM]
        vb_ref[:, (2 * h + 1) * V_DIM:(2 * h + 2) * V_DIM] = ones
    u_ref[...] = _dot(xn, w_ref[:, 3 * aw:])


def _in_proj(x2, g_mix, w_in_b, seg, gq_t, gk_t, tb):
    t, d = x2.shape
    n = w_in_b.shape[1]
    aw = seg.shape[0]
    sw = n - 3 * aw
    tb = min(tb, t)
    row = lambda w: pl.BlockSpec((tb, w), lambda i: (i, 0))
    return pl.pallas_call(
        _in_proj_body,
        grid=(t // tb,),
        in_specs=[row(d), _const_spec((1, d)), _const_spec((d, n)), _const_spec((aw, aw)),
                  _const_spec((1, aw)), _const_spec((1, aw))],
        out_specs=[row(aw), row(aw), row(aw), row(aw), row(2 * aw), row(sw)],
        out_shape=[jax.ShapeDtypeStruct((t, aw), BF16), jax.ShapeDtypeStruct((t, aw), F32),
                   jax.ShapeDtypeStruct((t, aw), BF16), jax.ShapeDtypeStruct((t, aw), F32),
                   jax.ShapeDtypeStruct((t, 2 * aw), BF16), jax.ShapeDtypeStruct((t, sw), F32)],
        compiler_params=_params("parallel"),
        name="in_proj",
    )(x2, g_mix, w_in_b, seg, gq_t, gk_t)


def _lambda(lq1_ref, lk1_ref, lq2_ref, lk2_ref, lambda_init):
    s1 = jnp.sum(lq1_ref[...] * lk1_ref[...], axis=-1, keepdims=True)
    s2 = jnp.sum(lq2_ref[...] * lk2_ref[...], axis=-1, keepdims=True)
    return jnp.exp(s1) - jnp.exp(s2) + lambda_init


def _stack_maps(q):
    lane = lax.broadcasted_iota(jnp.int32, q.shape, 1)
    zero = jnp.zeros_like(q)
    return jnp.concatenate([jnp.where(lane < QK_DIM, q, zero), jnp.where(lane >= QK_DIM, q, zero)], axis=0)


def _sub_norm(o2, l, lam, gsub, nq, lambda_init):
    o2 = o2 / l
    o = o2[:nq] - lam * o2[nq:]
    ms = jnp.mean(o * o, axis=-1, keepdims=True)
    return (o * lax.rsqrt(ms + NORM_EPS) * gsub) * (1.0 - lambda_init)


def _attn_prompt_body(bounded_ref, lq1_ref, lk1_ref, lq2_ref, lk2_ref, gsub_ref, q_ref, k_ref, v_ref, o_ref,
                      q2_scr, m_scr, acc_scr, *, tq, tk, rt, lambda_init):
    i = pl.program_id(1)
    q2_scr[...] = _stack_maps(q_ref[...])
    acc_scr[...] = jnp.zeros(acc_scr.shape, F32)
    jd = (i * tq) // tk

    def scores(key0, nkeys, r0, masked):
        keys = pl.ds(pl.multiple_of(key0, tq), nkeys)
        s = lax.dot_general(q2_scr[r0:r0 + rt, :], k_ref[keys, :], _NT, preferred_element_type=F32)
        if masked:
            row = lax.broadcasted_iota(jnp.int32, s.shape, 0)
            col = lax.broadcasted_iota(jnp.int32, s.shape, 1)
            q_chunk = (i * tq + (r0 % tq) + row) // CHUNK
            k_chunk = (key0 + col) // CHUNK
            s = jnp.where(k_chunk <= q_chunk, s, -jnp.inf)
        return s, keys

    def unshifted_step(key0, nkeys, masked):
        for r0 in range(0, 2 * tq, rt):
            s, keys = scores(key0, nkeys, r0, masked)
            acc_scr[r0:r0 + rt, :] += _dot(jnp.exp2(s).astype(BF16), v_ref[keys, :])

    def online_step(key0, nkeys, masked):
        for r0 in range(0, 2 * tq, rt):
            rows = slice(r0, r0 + rt)
            s, keys = scores(key0, nkeys, r0, masked)
            m_prev = m_scr[rows, :]
            m_new = jnp.maximum(m_prev, jnp.max(s, axis=1, keepdims=True))
            p = jnp.exp2(s - m_new)
            acc_scr[rows, :] = jnp.exp2(m_prev - m_new) * acc_scr[rows, :] + _dot(p.astype(BF16), v_ref[keys, :])
            m_scr[rows, :] = m_new

    def sweep(step):
        def full_step(j, c):
            step(j * tk, tk, False)
            return c
        lax.fori_loop(0, jd, full_step, 0)
        if tk > tq:
            pl.when(i * tq > jd * tk)(lambda: step(jd * tk, tq, False))
        step(i * tq, tq, True)

    @pl.when(bounded_ref[0] == 1)
    def _():
        sweep(unshifted_step)

    @pl.when(bounded_ref[0] != 1)
    def _():
        m_scr[...] = jnp.full(m_scr.shape, -jnp.inf, F32)
        sweep(online_step)

    lam = _lambda(lq1_ref, lk1_ref, lq2_ref, lk2_ref, lambda_init)
    acc = acc_scr[...]
    o_ref[...] = _sub_norm(acc[:, :V_DIM], acc[:, V_DIM:V_DIM + 1], lam, gsub_ref[...], tq, lambda_init)


def _attn_prompt(bounded, qb, kb, vb1, lam_parts, g_sub, lambda_init, tq, tk, rt):
    s, aw = qb.shape
    tk = min(tk, s)
    tq = min(tq, tk)
    rt = min(rt, tq)
    assert s % tk == 0 and tk in (tq, 2 * tq) and tq % rt == 0 and rt % CHUNK == 0
    lam_specs = [_const_spec((1, QK_DIM))] * 4
    return pl.pallas_call(
        functools.partial(_attn_prompt_body, tq=tq, tk=tk, rt=rt, lambda_init=lambda_init),
        grid=(ATT_HEADS, s // tq),
        in_specs=[pl.BlockSpec(memory_space=pltpu.SMEM)] + lam_specs + [
            _const_spec((1, V_DIM)), pl.BlockSpec((tq, V_DIM), lambda h, i: (i, h)),
            pl.BlockSpec((s, V_DIM), lambda h, i: (0, h)), pl.BlockSpec((s, 2 * V_DIM), lambda h, i: (0, h))],
        out_specs=pl.BlockSpec((tq, V_DIM), lambda h, i: (i, h)),
        out_shape=jax.ShapeDtypeStruct((s, aw), F32),
        scratch_shapes=[pltpu.VMEM((2 * tq, V_DIM), BF16), pltpu.VMEM((2 * tq, 1), F32),
                        pltpu.VMEM((2 * tq, 2 * V_DIM), F32)],
        compiler_params=_params("parallel", "arbitrary"),
        name="attn_prompt",
    )(bounded, *lam_parts, g_sub, qb, kb, vb1)


def _attn_sample_body(lq1_ref, lk1_ref, lq2_ref, lk2_ref, gsub_ref, q_ref, kp_ref, vp_ref, kn_ref, vn_ref,
                      o_ref, *, past, lambda_init):
    nq = q_ref.shape[1]
    q2 = _stack_maps(q_ref[0])
    s_past = lax.dot_general(q2, kp_ref[0], _NT, preferred_element_type=F32)
    s_new = lax.dot_general(q2, kn_ref[0], _NT, preferred_element_type=F32)
    row = lax.broadcasted_iota(jnp.int32, s_new.shape, 0)
    col = lax.broadcasted_iota(jnp.int32, s_new.shape, 1)
    q_chunk = (past + jnp.where(row >= nq, row - nq, row)) // CHUNK
    s_new = jnp.where((past + col) // CHUNK <= q_chunk, s_new, -jnp.inf)
    m = jnp.maximum(jnp.max(s_past, axis=1, keepdims=True), jnp.max(s_new, axis=1, keepdims=True))
    p_past = jnp.exp2(s_past - m)
    p_new = jnp.exp2(s_new - m)
    l = jnp.sum(p_past, axis=1, keepdims=True) + jnp.sum(p_new, axis=1, keepdims=True)
    acc = _dot(p_past.astype(BF16), vp_ref[0]) + _dot(p_new.astype(BF16), vn_ref[0])
    lam = _lambda(lq1_ref, lk1_ref, lq2_ref, lk2_ref, lambda_init)
    o_ref[0] = _sub_norm(acc, l, lam, gsub_ref[...], nq, lambda_init)


def _attn_sample(qb3, kp3, vp3, kb3, vb3, lam_parts, g_sub, lambda_init):
    b, s, aw = qb3.shape
    past = kp3.shape[1]
    assert past % CHUNK == 0
    lam_specs = [_const_spec((1, QK_DIM))] * 4
    new = pl.BlockSpec((1, s, V_DIM), lambda bi, h: (bi, 0, h))
    new_v = pl.BlockSpec((1, s, V_DIM), lambda bi, h: (bi, 0, 2 * h))
    old = pl.BlockSpec((1, past, V_DIM), lambda bi, h: (bi, 0, h))
    return pl.pallas_call(
        functools.partial(_attn_sample_body, past=past, lambda_init=lambda_init),
        grid=(b, ATT_HEADS),
        in_specs=lam_specs + [_const_spec((1, V_DIM)), new, old, old, new, new_v],
        out_specs=new,
        out_shape=jax.ShapeDtypeStruct((b, s, aw), F32),
        compiler_params=_params("parallel", "arbitrary"),
        name="attn_sample",
    )(*lam_parts, g_sub, qb3, kp3, vp3, kb3, vb3)


def _ssm_param_body(are_ref, aim_ref, ldt_ref, bre_ref, bim_ref, abr_ref, abi_ref, bbr_ref, bbi_ref):
    ar, ai = are_ref[...], aim_ref[...]
    dt = jnp.exp(ldt_ref[...])
    mag = jnp.exp(ar * dt)
    abr, abi = mag * jnp.cos(ai * dt), mag * jnp.sin(ai * dt)
    den = ar * ar + ai * ai
    nr, ni = abr - 1.0, abi
    fr, fi = (nr * ar + ni * ai) / den, (ni * ar - nr * ai) / den
    br, bi = bre_ref[...], bim_ref[...]
    abr_ref[...] = abr
    abi_ref[...] = abi
    bbr_ref[...] = fr * br - fi * bi
    bbi_ref[...] = fr * bi + fi * br


def _ssm_params(a_re, a_im, log_dt, b_re, b_im, c_re, c_im):
    g, p = a_re.shape
    c = b_re.shape[-1]
    n = g * p
    col = lambda a: a.reshape(n, 1)
    abr, abi, bbr, bbi = pl.pallas_call(
        _ssm_param_body,
        out_shape=[jax.ShapeDtypeStruct((n, 1), F32)] * 2 + [jax.ShapeDtypeStruct((n, c), F32)] * 2,
        name="ssm_params",
    )(col(a_re), col(a_im), col(jnp.repeat(log_dt, p)), b_re.reshape(n, c), b_im.reshape(n, c))
    ns = g // SSM_SUPER
    eye = jnp.eye(SSM_SUPER, dtype=F32)

    def in_blocks(bb):
        bb = bb.reshape(ns, SSM_SUPER, p, c)
        return jnp.einsum('sgpc,gh->sgchp', bb, eye).reshape(ns, SSM_SUPER * c, SSM_SLAB)

    def out_blocks(cc):
        cc = cc.reshape(ns, SSM_SUPER, c, p)
        return jnp.einsum('sgcp,gh->sgphc', cc, eye).reshape(ns, SSM_SLAB, SSM_SUPER * c)

    bw = jnp.concatenate([in_blocks(bbr), in_blocks(bbi)], axis=2).astype(BF16)
    cw = jnp.concatenate([out_blocks(c_re), -out_blocks(c_im)], axis=1).astype(BF16)
    pr, pi = [abr.reshape(ns, 1, SSM_SLAB)], [abi.reshape(ns, 1, SSM_SLAB)]
    for _ in range(SUBLANES - 1):
        pr, pi = (pr + [pr[-1] * pr[0] - pi[-1] * pi[0]], pi + [pr[-1] * pi[0] + pi[-1] * pr[0]])
    rows = jnp.arange(SUBLANES).reshape(1, SUBLANES, 1)
    tabs = []
    for k in (1, 2, 4):
        tabs += [jnp.where(rows >= k, pr[k - 1], 0.0), jnp.where(rows >= k, pi[k - 1], 0.0)]
    tabs += [jnp.concatenate(pr, axis=1), jnp.concatenate(pi, axis=1)]
    return bw, cw, jnp.stack(tabs, axis=1)


def _ssm_body(u_ref, bw_ref, cw_ref, pw_ref, dsk_ref, h0r_ref, h0i_ref, y_ref, hr_ref, hi_ref,
              h_scr, c_scr):
    tb = pl.program_id(2)
    nrow = u_ref.shape[1]
    ns = SSM_SLAB

    @pl.when(tb == 0)
    def _():
        c_scr[0:1, :] = h0r_ref[0]
        c_scr[1:2, :] = h0i_ref[0]

    u = u_ref[0]
    h_scr[...] = _dot(u.astype(BF16), bw_ref[0])
    t1r, t1i, t2r, t2i, t4r, t4i, tcr, tci = (pw_ref[0, n] for n in range(8))

    def body(r, carry):
        cr, ci = carry
        rows = pl.ds(pl.multiple_of(r * SUBLANES, SUBLANES), SUBLANES)
        xr, xi = h_scr[rows, 0:ns], h_scr[rows, ns:2 * ns]
        for k, pr, pi in ((1, t1r, t1i), (2, t2r, t2i), (4, t4r, t4i)):
            sr, si = pltpu.roll(xr, k, 0), pltpu.roll(xi, k, 0)
            xr, xi = xr + pr * sr - pi * si, xi + pr * si + pi * sr
        xr, xi = xr + tcr * cr - tci * ci, xi + tcr * ci + tci * cr
        h_scr[rows, 0:ns] = xr
        h_scr[rows, ns:2 * ns] = xi
        return xr[SUBLANES - 1:SUBLANES], xi[SUBLANES - 1:SUBLANES]

    cr, ci = lax.fori_loop(0, nrow // SUBLANES, body, (c_scr[0:1, :], c_scr[1:2, :]))
    c_scr[0:1, :] = cr
    c_scr[1:2, :] = ci
    y_ref[0] = _dot(h_scr[...].astype(BF16), cw_ref[0]) + dsk_ref[...] * u

    @pl.when(tb == pl.num_programs(2) - 1)
    def _():
        hr_ref[0] = cr
        hi_ref[0] = ci


def _ssm(u3, h0r, h0i, bw, cw, pw, d_skip, tb):
    b, s, sw = u3.shape
    ns = sw // LANES
    tb = min(tb, s)
    assert s % tb == 0 and tb % SUBLANES == 0
    nstate = ns * SSM_SLAB
    slab = lambda r, c: pl.BlockSpec((1, r, c), lambda bi, g, t: (g, 0, 0))
    state = pl.BlockSpec((1, 1, SSM_SLAB), lambda bi, g, t: (bi, 0, g))
    seq = pl.BlockSpec((1, tb, LANES), lambda bi, g, t: (bi, t, g))
    return pl.pallas_call(
        _ssm_body,
        grid=(b, ns, s // tb),
        in_specs=[seq, slab(LANES, 2 * SSM_SLAB), slab(2 * SSM_SLAB, LANES),
                  pl.BlockSpec((1, 8, SUBLANES, SSM_SLAB), lambda bi, g, t: (g, 0, 0, 0)),
                  pl.BlockSpec((1, LANES), lambda bi, g, t: (0, g)), state, state],
        out_specs=[seq, state, state],
        out_shape=[jax.ShapeDtypeStruct((b, s, sw), F32), jax.ShapeDtypeStruct((b, 1, nstate), F32),
                   jax.ShapeDtypeStruct((b, 1, nstate), F32)],
        scratch_shapes=[pltpu.VMEM((tb, 2 * SSM_SLAB), F32), pltpu.VMEM((SUBLANES, SSM_SLAB), F32)],
        compiler_params=_params("parallel", "parallel", "arbitrary"),
        name="ssm",
    )(u3, bw, cw, pw, d_skip, h0r.reshape(b, 1, nstate), h0i.reshape(b, 1, nstate))


def _post_body(y_ref, o_ref, x_ref, wglu_ref, bglu_ref, gssm_ref, wout_ref, gffn_ref, wq_ref,
               h_ref, xq_ref, qp_ref):
    aw = o_ref.shape[-1]
    z = _gelu(y_ref[...])
    z = z * jax.nn.sigmoid(_dot(z.astype(BF16), wglu_ref[...]) + bglu_ref[...])
    ys = z * lax.rsqrt(jnp.mean(z * z, axis=-1, keepdims=True) + NORM_EPS) * gssm_ref[...]
    h = (x_ref[...] + _dot(o_ref[...].astype(BF16), wout_ref[0:aw, :])
         + _dot(ys.astype(BF16), wout_ref[aw:, :]))
    h_ref[...] = h
    xq = (h * lax.rsqrt(jnp.mean(h * h, axis=-1, keepdims=True) + NORM_EPS) * gffn_ref[...]).astype(BF16)
    xq_ref[...] = xq
    qp_ref[...] = _dot(xq, wq_ref[...]).astype(BF16)


def _post(y, o, x2, w_glu_b, b_glu, g_ssm, w_out_b, g_ffn, w_query_b, tb):
    t, d = x2.shape
    aw, sw = o.shape[1], y.shape[1]
    nq = w_query_b.shape[1]
    tb = min(tb, t)
    row = lambda w: pl.BlockSpec((tb, w), lambda i: (i, 0))
    return pl.pallas_call(
        _post_body,
        grid=(t // tb,),
        in_specs=[row(sw), row(aw), row(d), _const_spec((sw, sw)), _const_spec((1, sw)), _const_spec((1, sw)),
                  _const_spec((d, d)), _const_spec((1, d)), _const_spec((d, nq))],
        out_specs=[row(d), row(d), row(nq)],
        out_shape=[jax.ShapeDtypeStruct((t, d), F32), jax.ShapeDtypeStruct((t, d), BF16),
                   jax.ShapeDtypeStruct((t, nq), BF16)],
        compiler_params=_params("parallel"),
        name="post",
    )(y, o, x2, w_glu_b, b_glu, g_ssm, w_out_b, g_ffn, w_query_b)


def _top_rows(x, k):
    nrows = x.shape[0]
    rid = lax.broadcasted_iota(jnp.int32, x.shape, 0)
    rank = jnp.full(x.shape, float(k), F32)
    vals = []
    for r in range(k):
        m = jnp.max(x, axis=0, keepdims=True)
        first = jnp.min(jnp.where(x == m, rid, nrows), axis=0, keepdims=True)
        hit = rid == first
        x = jnp.where(hit, -jnp.inf, x)
        rank = jnp.where(hit, float(r), rank)
        vals.append(m)
    return vals, rank


def _peer_select_body(qp_ref, sk_ref, e0_ref, n0_ref, c1_ref, e1_ref):
    nk = PEER_NKEYS
    s0 = lax.dot_general(sk_ref[0, 0], qp_ref[:, 0:nk], _NT, preferred_element_type=F32)
    s1 = lax.dot_general(sk_ref[0, 1], qp_ref[:, nk:2 * nk], _NT, preferred_element_type=F32)
    a, rank0 = _top_rows(s0, PEER_TOPK)
    b, rank1 = _top_rows(s1, PEER_TOPK)
    width = [PEER_TOPK // (r + 1) for r in range(PEER_TOPK)]
    cand = [[a[r] + b[c] for c in range(width[r])] for r in range(PEER_TOPK)]
    flat = [x for row in cand for x in row]
    pad = -len(flat) % SUBLANES
    f, _ = _top_rows(jnp.concatenate(flat + [jnp.full_like(a[0], -jnp.inf)] * pad, axis=0), PEER_TOPK)
    tau = f[PEER_TOPK - 1]
    z = jnp.exp(f[0] - f[0])
    for fv in f[1:]:
        z = z + jnp.exp(fv - f[0])
    n0 = jnp.zeros(s0.shape, F32)
    for r in range(PEER_TOPK):
        n_r = jnp.zeros_like(tau)
        for c in range(width[r]):
            n_r = n_r + jnp.where(cand[r][c] >= tau, 1.0, 0.0)
        n0 = jnp.where(rank0 == float(r), n_r, n0)
    n0_ref[0] = n0
    c1_ref[0] = rank1
    e0_ref[0] = jnp.exp(s0 - a[0])
    e1_ref[0] = jnp.exp(s1 - b[0]) / z


def _peer_select(qp, sub_keys_b, tb):
    t = qp.shape[0]
    nk = PEER_NKEYS
    tb = min(tb, t)
    tile = pl.BlockSpec((1, nk, tb), lambda i, h: (h, 0, i))
    full = jax.ShapeDtypeStruct((PEER_HEADS, nk, t), F32)
    return pl.pallas_call(
        _peer_select_body,
        grid=(t // tb, PEER_HEADS),
        in_specs=[pl.BlockSpec((tb, 2 * nk), lambda i, h: (i, h)),
                  pl.BlockSpec((1, 2, nk, nk), lambda i, h: (h, 0, 0, 0))],
        out_specs=[tile, tile, tile, tile],
        out_shape=[full, full, full, full],
        compiler_params=_params("parallel", "parallel"),
        name="peer_select",
    )(qp, sub_keys_b)


def _peer_body(xq_ref, e0_ref, n0_ref, c1_ref, e1_ref, u_ref, vt_ref, h_ref, y_ref,
               acc_scr, wt_scr, *, nblocks, pieces):
    e = pl.program_id(1)
    nk = PEER_NKEYS
    eb, d = u_ref.shape
    tb = xq_ref.shape[0]
    per_block = eb // nk
    rows_a, rows_c = eb // pieces, d // pieces

    @pl.when(e == 0)
    def _():
        acc_scr[...] = jnp.zeros(acc_scr.shape, F32)
        wt_scr[...] = jnp.zeros(wt_scr.shape, BF16)

    def step(cur, prev):
        n_rows = [n0_ref[h] for h in range(PEER_HEADS)]
        e_rows = [e0_ref[h] for h in range(PEER_HEADS)]
        tw = min(MXU_DIM, tb)
        for t0 in range(0, tb, tw):
            toks = slice(t0, t0 + tw)
            for p in range(pieces):
                hid = lax.dot_general(u_ref[p * rows_a:(p + 1) * rows_a, :], xq_ref[toks, :], _NT,
                                      preferred_element_type=F32)
                slab = slice(p * rows_c, (p + 1) * rows_c)
                acc_scr[slab, toks] += _dot(vt_ref[slab, :], wt_scr[prev, :, toks])
                for il in range(rows_a // nk):
                    i = p * (rows_a // nk) + il
                    for c in range(0, tw, LANES):
                        lanes = slice(t0 + c, t0 + c + LANES)
                        for j0 in range(0, nk, GATE_ROWS):
                            keys = slice(j0, j0 + GATE_ROWS)
                            gate = jnp.zeros((GATE_ROWS, LANES), F32)
                            for h in range(PEER_HEADS):
                                sel = c1_ref[h, keys, lanes] < n_rows[h][i:i + 1, lanes]
                                gate = gate + jnp.where(sel, e_rows[h][i:i + 1, lanes] * e1_ref[h, keys, lanes], 0.0)
                            act = _gelu(hid[il * nk + j0:il * nk + j0 + GATE_ROWS, c:c + LANES])
                            rows = slice(p * rows_a + il * nk + j0, p * rows_a + il * nk + j0 + GATE_ROWS)
                            wt_scr[cur, rows, lanes] = (gate * act).astype(BF16)

    for parity in (0, 1):
        pl.when(e % 2 == parity)(functools.partial(step, parity, 1 - parity))

    @pl.when(e == pl.num_programs(1) - 1)
    def _():
        y_ref[...] = h_ref[...] + acc_scr[...].T


def _peer(xq, sel, u_b, vt_b, h, tb, eb):
    t, d = xq.shape
    ne = u_b.shape[0]
    nk = PEER_NKEYS
    tb = min(tb, t)
    nblocks = ne // eb
    single = pl.Buffered(1)
    tile = pl.BlockSpec((PEER_HEADS, nk, tb), lambda i, e: (0, 0, i), pipeline_mode=single)
    rows = pl.BlockSpec((PEER_HEADS, eb // nk, tb), lambda i, e: (0, jnp.minimum(e, nblocks - 1), i))
    return pl.pallas_call(
        functools.partial(_peer_body, nblocks=nblocks, pieces=4),
        grid=(t // tb, nblocks + 1),
        in_specs=[pl.BlockSpec((tb, d), lambda i, e: (i, 0), pipeline_mode=single),
                  rows, rows, tile, tile,
                  pl.BlockSpec((eb, d), lambda i, e: (jnp.minimum(e, nblocks - 1), 0)),
                  pl.BlockSpec((d, eb), lambda i, e: (0, jnp.maximum(e - 1, 0))),
                  pl.BlockSpec((tb, d), lambda i, e: (i, 0), pipeline_mode=single)],
        out_specs=pl.BlockSpec((tb, d), lambda i, e: (i, 0)),
        out_shape=jax.ShapeDtypeStruct((t, d), F32),
        scratch_shapes=[pltpu.VMEM((d, tb), F32), pltpu.VMEM((2, eb, tb), BF16)],
        compiler_params=_params("parallel", "arbitrary"),
        name="peer",
    )(xq, *sel, u_b, vt_b, h)


def _layer(x, k_past, v_past, h0_re, h0_im, lambda_init, w):
    b, s, d = x.shape
    t = b * s
    aw = ATT_HEADS * V_DIM
    x2 = x.reshape(t, d)
    qb, k, kb, v, vb, u = _in_proj(x2, w['g_mix'], w['w_in'], w['seg'], w['gq'], w['gk'], tb=256)
    if k_past is None:
        assert b == 1
        o = _attn_prompt(w['bounded'], qb, kb, vb, w['lam'], w['g_sub'], lambda_init, tq=512, tk=1024, rt=256)
        h0_re = jnp.zeros((b, w['nstate']), F32)
        h0_im = h0_re
        ssm_tb = 512
    else:
        past = k_past.shape[1]
        r3 = lambda a: a.reshape(b, s, aw)
        o = _attn_sample(r3(qb), k_past.astype(BF16).reshape(b, past, aw), v_past.astype(BF16).reshape(b, past, aw), r3(kb),
                         vb.reshape(b, s, 2 * aw), w['lam'], w['g_sub'], lambda_init).reshape(t, aw)
        ssm_tb = s
    y, h_re, h_im = _ssm(u.reshape(b, s, -1), h0_re.reshape(b, -1), h0_im.reshape(b, -1),
                         w['bw'], w['cw'], w['pw'], w['d_skip'], tb=ssm_tb)
    hres, xq, qp = _post(y.reshape(t, -1), o, x2, w['w_glu'], w['b_glu'], w['g_ssm'], w['w_out'], w['g_ffn'],
                         w['w_query'], tb=256)
    sel = _peer_select(qp, w['sub_keys'], tb=512)
    out = _peer(xq, sel, w['u_table'], w['v_table_t'], hres, tb=512, eb=1024)
    ngroups = w['nstate'] // SSM_STATE
    return (out.reshape(b, s, d), k.reshape(b, s, ATT_HEADS, 2, QK_DIM), v.reshape(b, s, ATT_HEADS, V_DIM),
            h_re.reshape(b, ngroups, SSM_STATE), h_im.reshape(b, ngroups, SSM_STATE))


def _prep_weights(l, g_mix, w_in, g_q, g_k, lam_q1, lam_k1, lam_q2, lam_k2, g_sub, a_re, a_im, log_dt, b_re,
                  b_im, c_re, c_im, d_skip, w_glu, b_glu, g_ssm, w_out, g_ffn, w_query, sub_keys, u_table,
                  v_table):
    aw = ATT_HEADS * V_DIM
    row = lambda a: a[l].reshape(1, -1)
    seg_id = jnp.arange(aw) // QK_DIM
    seg = jnp.where(seg_id[:, None] == seg_id[None, :], 1.0 / QK_DIM, 0.0).astype(BF16)
    bw, cw, pw = _ssm_params(a_re[l], a_im[l], log_dt[l], b_re[l], b_im[l], c_re[l], c_im[l])
    q_scale = QK_DIM ** -0.5 * LOG2_E
    score_bound = QK_DIM * q_scale * jnp.max(jnp.abs(g_q[l])) * jnp.max(jnp.abs(g_k[l])) * BF16_SLACK
    return dict(
        bounded=(score_bound <= MAX_UNSHIFTED_LOG2).astype(jnp.int32).reshape(1),
        g_mix=row(g_mix), w_in=w_in[l].astype(BF16), seg=seg,
        gq=jnp.tile(g_q[l], aw // QK_DIM).reshape(1, aw) * q_scale,
        gk=jnp.tile(g_k[l], aw // QK_DIM).reshape(1, aw),
        lam=(row(lam_q1), row(lam_k1), row(lam_q2), row(lam_k2)), g_sub=row(g_sub),
        bw=bw, cw=cw, pw=pw, d_skip=row(d_skip), nstate=a_re.shape[1] * a_re.shape[2],
        w_glu=w_glu[l].astype(BF16), b_glu=row(b_glu), g_ssm=row(g_ssm), w_out=w_out[l].astype(BF16),
        g_ffn=row(g_ffn), w_query=w_query[l].astype(BF16), sub_keys=sub_keys[l].astype(BF16),
        u_table=u_table[l].astype(BF16), v_table_t=v_table[l].T.astype(BF16))


def kernel(x_prompt, x_sample, cache_k, cache_v, state_ssm_re, state_ssm_im, g_mix, w_in, g_q, g_k, lam_q1, lam_k1, lam_q2, lam_k2, g_sub, a_re, a_im, log_dt, b_re, b_im, c_re, c_im, d_skip, w_glu, b_glu, g_ssm, w_out, g_ffn, w_query, sub_keys, u_table, v_table):
    depth = w_in.shape[0]
    yp, ys = x_prompt, x_sample
    outs = [[] for _ in range(8)]
    for l in range(depth):
        lambda_init = 0.8 - 0.6 * math.exp(-0.3 * l)
        w = _prep_weights(l, g_mix, w_in, g_q, g_k, lam_q1, lam_k1, lam_q2, lam_k2, g_sub, a_re, a_im, log_dt,
                          b_re, b_im, c_re, c_im, d_skip, w_glu, b_glu, g_ssm, w_out, g_ffn, w_query, sub_keys,
                          u_table, v_table)
        yp, kp, vp, hrp, hip = _layer(yp, None, None, None, None, lambda_init, w)
        ys, kk, vv, hrs, his = _layer(ys, cache_k[l], cache_v[l], state_ssm_re[l], state_ssm_im[l],
                                      lambda_init, w)
        for acc, val in zip(outs, (kp, vp, hrp, hip, kk, vv, hrs, his)):
            acc.append(val)
    return (yp, ys) + tuple(jnp.stack(o) for o in outs)
```

```python
import functools
import math

import jax
import jax.numpy as jnp
from jax import lax
from jax.experimental import pallas as pl
from jax.experimental.pallas import tpu as pltpu

F32 = jnp.float32
BF16 = jnp.bfloat16

CHUNK = 64
ATT_HEADS = 8
QK_DIM = 64
V_DIM = 2 * QK_DIM
SSM_GROUP = 16
SSM_STATE = 64
PEER_HEADS = 8
PEER_NKEYS = 128
PEER_TOPK = 16
NORM_EPS = 1e-6

LANES = 128
SUBLANES = 8
MXU_DIM = 256
GATE_ROWS = 32
SSM_SUPER = LANES // SSM_GROUP
SSM_SLAB = SSM_SUPER * SSM_STATE
VMEM_LIMIT = 56 * 1024 * 1024
SQRT_HALF = math.sqrt(0.5)
LOG2_E = math.log2(math.e)
BF16_SLACK = 1.0 + 2.0 ** -6
MAX_UNSHIFTED_LOG2 = 60.0

_NT = (((1,), (1,)), ((), ()))


def _params(*sem):
    return pltpu.CompilerParams(dimension_semantics=sem, vmem_limit_bytes=VMEM_LIMIT)


def _const_spec(shape):
    nd = len(shape)
    return pl.BlockSpec(shape, lambda *_: (0,) * nd, pipeline_mode=pl.Buffered(1))


def _gelu(x):
    return 0.5 * x * (1.0 + lax.erf(x * SQRT_HALF))


def _dot(a, b):
    return jnp.dot(a, b, preferred_element_type=F32)


def _in_proj_body(x_ref, gmix_ref, w_ref, seg_ref, gq_ref, gk_ref,
                  qb_ref, k_ref, kb_ref, v_ref, vb_ref, u_ref):
    aw = k_ref.shape[-1]
    x = x_ref[...]
    ms = jnp.mean(x * x, axis=-1, keepdims=True)
    xn = (x * lax.rsqrt(ms + NORM_EPS) * gmix_ref[...]).astype(BF16)

    def seg_rms(y, g):
        sq = y * y
        hi = sq.astype(BF16)
        lo = (sq - hi.astype(F32)).astype(BF16)
        mean = _dot(hi, seg_ref[...]) + _dot(lo, seg_ref[...])
        return y * lax.rsqrt(mean + NORM_EPS) * g

    q = seg_rms(_dot(xn, w_ref[:, 0:aw]), gq_ref[...])
    qb_ref[...] = q.astype(BF16)
    k = seg_rms(_dot(xn, w_ref[:, aw:2 * aw]), gk_ref[...])
    k_ref[...] = k
    kb_ref[...] = k.astype(BF16)
    v = _dot(xn, w_ref[:, 2 * aw:3 * aw])
    v_ref[...] = v
    vb = v.astype(BF16)
    ones = jnp.ones((vb.shape[0], V_DIM), BF16)
    for h in range(aw // V_DIM):
        vb_ref[:, 2 * h * V_DIM:(2 * h + 1) * V_DIM] = vb[:, h * V_DIM:(h + 1) * V_DIM]
        vb_ref[:, (2 * h + 1) * V_DIM:(2 * h + 2) * V_DIM] = ones
    u_ref[...] = _dot(xn, w_ref[:, 3 * aw:])


def _in_proj(x2, g_mix, w_in_b, seg, gq_t, gk_t, tb):
    t, d = x2.shape
    n = w_in_b.shape[1]
    aw = seg.shape[0]
    sw = n - 3 * aw
    tb = min(tb, t)
    row = lambda w: pl.BlockSpec((tb, w), lambda i: (i, 0))
    return pl.pallas_call(
        _in_proj_body,
        grid=(t // tb,),
        in_specs=[row(d), _const_spec((1, d)), _const_spec((d, n)), _const_spec((aw, aw)),
                  _const_spec((1, aw)), _const_spec((1, aw))],
        out_specs=[row(aw), row(aw), row(aw), row(aw), row(2 * aw), row(sw)],
        out_shape=[jax.ShapeDtypeStruct((t, aw), BF16), jax.ShapeDtypeStruct((t, aw), F32),
                   jax.ShapeDtypeStruct((t, aw), BF16), jax.ShapeDtypeStruct((t, aw), F32),
                   jax.ShapeDtypeStruct((t, 2 * aw), BF16), jax.ShapeDtypeStruct((t, sw), F32)],
        compiler_params=_params("parallel"),
        name="in_proj",
    )(x2, g_mix, w_in_b, seg, gq_t, gk_t)


def _lambda(lq1_ref, lk1_ref, lq2_ref, lk2_ref, lambda_init):
    s1 = jnp.sum(lq1_ref[...] * lk1_ref[...], axis=-1, keepdims=True)
    s2 = jnp.sum(lq2_ref[...] * lk2_ref[...], axis=-1, keepdims=True)
    return jnp.exp(s1) - jnp.exp(s2) + lambda_init


def _stack_maps(q):
    lane = lax.broadcasted_iota(jnp.int32, q.shape, 1)
    zero = jnp.zeros_like(q)
    return jnp.concatenate([jnp.where(lane < QK_DIM, q, zero), jnp.where(lane >= QK_DIM, q, zero)], axis=0)


def _sub_norm(o2, l, lam, gsub, nq, lambda_init):
    o2 = o2 / l
    o = o2[:nq] - lam * o2[nq:]
    ms = jnp.mean(o * o, axis=-1, keepdims=True)
    return (o * lax.rsqrt(ms + NORM_EPS) * gsub) * (1.0 - lambda_init)


def _attn_prompt_body(bounded_ref, lq1_ref, lk1_ref, lq2_ref, lk2_ref, gsub_ref, q_ref, k_ref, v_ref, o_ref,
                      q2_scr, m_scr, acc_scr, *, tq, tk, rt, lambda_init):
    i = pl.program_id(1)
    q2_scr[...] = _stack_maps(q_ref[...])
    acc_scr[...] = jnp.zeros(acc_scr.shape, F32)
    jd = (i * tq) // tk

    def scores(key0, nkeys, r0, masked):
        keys = pl.ds(pl.multiple_of(key0, tq), nkeys)
        s = lax.dot_general(q2_scr[r0:r0 + rt, :], k_ref[keys, :], _NT, preferred_element_type=F32)
        if masked:
            row = lax.broadcasted_iota(jnp.int32, s.shape, 0)
            col = lax.broadcasted_iota(jnp.int32, s.shape, 1)
            q_chunk = (i * tq + (r0 % tq) + row) // CHUNK
            k_chunk = (key0 + col) // CHUNK
            s = jnp.where(k_chunk <= q_chunk, s, -jnp.inf)
        return s, keys

    def unshifted_step(key0, nkeys, masked):
        for r0 in range(0, 2 * tq, rt):
            s, keys = scores(key0, nkeys, r0, masked)
            acc_scr[r0:r0 + rt, :] += _dot(jnp.exp2(s).astype(BF16), v_ref[keys, :])

    def online_step(key0, nkeys, masked):
        for r0 in range(0, 2 * tq, rt):
            rows = slice(r0, r0 + rt)
            s, keys = scores(key0, nkeys, r0, masked)
            m_prev = m_scr[rows, :]
            m_new = jnp.maximum(m_prev, jnp.max(s, axis=1, keepdims=True))
            p = jnp.exp2(s - m_new)
            acc_scr[rows, :] = jnp.exp2(m_prev - m_new) * acc_scr[rows, :] + _dot(p.astype(BF16), v_ref[keys, :])
            m_scr[rows, :] = m_new

    def sweep(step):
        def full_step(j, c):
            step(j * tk, tk, False)
            return c
        lax.fori_loop(0, jd, full_step, 0)
        if tk > tq:
            pl.when(i * tq > jd * tk)(lambda: step(jd * tk, tq, False))
        step(i * tq, tq, True)

    @pl.when(bounded_ref[0] == 1)
    def _():
        sweep(unshifted_step)

    @pl.when(bounded_ref[0] != 1)
    def _():
        m_scr[...] = jnp.full(m_scr.shape, -jnp.inf, F32)
        sweep(online_step)

    lam = _lambda(lq1_ref, lk1_ref, lq2_ref, lk2_ref, lambda_init)
    acc = acc_scr[...]
    o_ref[...] = _sub_norm(acc[:, :V_DIM], acc[:, V_DIM:V_DIM + 1], lam, gsub_ref[...], tq, lambda_init)


def _attn_prompt(bounded, qb, kb, vb1, lam_parts, g_sub, lambda_init, tq, tk, rt):
    s, aw = qb.shape
    tk = min(tk, s)
    tq = min(tq, tk)
    rt = min(rt, tq)
    assert s % tk == 0 and tk in (tq, 2 * tq) and tq % rt == 0 and rt % CHUNK == 0
    lam_specs = [_const_spec((1, QK_DIM))] * 4
    return pl.pallas_call(
        functools.partial(_attn_prompt_body, tq=tq, tk=tk, rt=rt, lambda_init=lambda_init),
        grid=(ATT_HEADS, s // tq),
        in_specs=[pl.BlockSpec(memory_space=pltpu.SMEM)] + lam_specs + [
            _const_spec((1, V_DIM)), pl.BlockSpec((tq, V_DIM), lambda h, i: (i, h)),
            pl.BlockSpec((s, V_DIM), lambda h, i: (0, h)), pl.BlockSpec((s, 2 * V_DIM), lambda h, i: (0, h))],
        out_specs=pl.BlockSpec((tq, V_DIM), lambda h, i: (i, h)),
        out_shape=jax.ShapeDtypeStruct((s, aw), F32),
        scratch_shapes=[pltpu.VMEM((2 * tq, V_DIM), BF16), pltpu.VMEM((2 * tq, 1), F32),
                        pltpu.VMEM((2 * tq, 2 * V_DIM), F32)],
        compiler_params=_params("parallel", "arbitrary"),
        name="attn_prompt",
    )(bounded, *lam_parts, g_sub, qb, kb, vb1)


def _attn_sample_body(lq1_ref, lk1_ref, lq2_ref, lk2_ref, gsub_ref, q_ref, kp_ref, vp_ref, kn_ref, vn_ref,
                      o_ref, *, past, lambda_init):
    nq = q_ref.shape[1]
    q2 = _stack_maps(q_ref[0])
    s_past = _dot(q2, kp_ref[0, 0].astype(BF16))
    v_past = vp_ref[0, pl.ds(pl.program_id(1), past, stride=ATT_HEADS), :]
    s_new = lax.dot_general(q2, kn_ref[0], _NT, preferred_element_type=F32)
    row = lax.broadcasted_iota(jnp.int32, s_new.shape, 0)
    col = lax.broadcasted_iota(jnp.int32, s_new.shape, 1)
    q_chunk = (past + jnp.where(row >= nq, row - nq, row)) // CHUNK
    s_new = jnp.where((past + col) // CHUNK <= q_chunk, s_new, -jnp.inf)
    m = jnp.maximum(jnp.max(s_past, axis=1, keepdims=True), jnp.max(s_new, axis=1, keepdims=True))
    p_past = jnp.exp2(s_past - m)
    p_new = jnp.exp2(s_new - m)
    l = jnp.sum(p_past, axis=1, keepdims=True) + jnp.sum(p_new, axis=1, keepdims=True)
    acc = _dot(p_past.astype(BF16), v_past.astype(BF16)) + _dot(p_new.astype(BF16), vn_ref[0])
    lam = _lambda(lq1_ref, lk1_ref, lq2_ref, lk2_ref, lambda_init)
    o_ref[0] = _sub_norm(acc, l, lam, gsub_ref[...], nq, lambda_init)


def _attn_sample(qb3, kpt, vp3, kb3, vb3, lam_parts, g_sub, lambda_init):
    b, s, aw = qb3.shape
    past = kpt.shape[-1]
    assert past % CHUNK == 0
    lam_specs = [_const_spec((1, QK_DIM))] * 4
    new = pl.BlockSpec((1, s, V_DIM), lambda bi, h: (bi, 0, h))
    new_v = pl.BlockSpec((1, s, V_DIM), lambda bi, h: (bi, 0, 2 * h))
    old_k = pl.BlockSpec((1, 1, 2 * QK_DIM, past), lambda bi, h: (bi, h, 0, 0))
    old_v = pl.BlockSpec((1, past * ATT_HEADS, V_DIM), lambda bi, h: (bi, 0, 0))
    return pl.pallas_call(
        functools.partial(_attn_sample_body, past=past, lambda_init=lambda_init),
        grid=(b, ATT_HEADS),
        in_specs=lam_specs + [_const_spec((1, V_DIM)), new, old_k, old_v, new, new_v],
        out_specs=new,
        out_shape=jax.ShapeDtypeStruct((b, s, aw), F32),
        compiler_params=_params("parallel", "arbitrary"),
        name="attn_sample",
    )(*lam_parts, g_sub, qb3, kpt, vp3, kb3, vb3)


def _ssm_param_body(are_ref, aim_ref, ldt_ref, bre_ref, bim_ref, abr_ref, abi_ref, bbr_ref, bbi_ref):
    ar, ai = are_ref[...], aim_ref[...]
    dt = jnp.exp(ldt_ref[...])
    mag = jnp.exp(ar * dt)
    abr, abi = mag * jnp.cos(ai * dt), mag * jnp.sin(ai * dt)
    den = ar * ar + ai * ai
    nr, ni = abr - 1.0, abi
    fr, fi = (nr * ar + ni * ai) / den, (ni * ar - nr * ai) / den
    br, bi = bre_ref[...], bim_ref[...]
    abr_ref[...] = abr
    abi_ref[...] = abi
    bbr_ref[...] = fr * br - fi * bi
    bbi_ref[...] = fr * bi + fi * br


def _ssm_params(a_re, a_im, log_dt, b_re, b_im, c_re, c_im):
    g, p = a_re.shape
    c = b_re.shape[-1]
    n = g * p
    col = lambda a: a.reshape(n, 1)
    abr, abi, bbr, bbi = pl.pallas_call(
        _ssm_param_body,
        out_shape=[jax.ShapeDtypeStruct((n, 1), F32)] * 2 + [jax.ShapeDtypeStruct((n, c), F32)] * 2,
        name="ssm_params",
    )(col(a_re), col(a_im), col(jnp.repeat(log_dt, p)), b_re.reshape(n, c), b_im.reshape(n, c))
    ns = g // SSM_SUPER
    eye = jnp.eye(SSM_SUPER, dtype=F32)

    def in_blocks(bb):
        bb = bb.reshape(ns, SSM_SUPER, p, c)
        return jnp.einsum('sgpc,gh->sgchp', bb, eye).reshape(ns, SSM_SUPER * c, SSM_SLAB)

    def out_blocks(cc):
        cc = cc.reshape(ns, SSM_SUPER, c, p)
        return jnp.einsum('sgcp,gh->sgphc', cc, eye).reshape(ns, SSM_SLAB, SSM_SUPER * c)

    bw = jnp.concatenate([in_blocks(bbr), in_blocks(bbi)], axis=2).astype(BF16)
    cw = jnp.concatenate([out_blocks(c_re), -out_blocks(c_im)], axis=1).astype(BF16)
    pr, pi = [abr.reshape(ns, 1, SSM_SLAB)], [abi.reshape(ns, 1, SSM_SLAB)]
    for _ in range(SUBLANES - 1):
        pr, pi = (pr + [pr[-1] * pr[0] - pi[-1] * pi[0]], pi + [pr[-1] * pi[0] + pi[-1] * pr[0]])
    rows = jnp.arange(SUBLANES).reshape(1, SUBLANES, 1)
    tabs = []
    for k in (1, 2, 4):
        tabs += [jnp.where(rows >= k, pr[k - 1], 0.0), jnp.where(rows >= k, pi[k - 1], 0.0)]
    tabs += [jnp.concatenate(pr, axis=1), jnp.concatenate(pi, axis=1)]
    return bw, cw, jnp.stack(tabs, axis=1)


def _ssm_body(u_ref, bw_ref, cw_ref, pw_ref, dsk_ref, h0r_ref, h0i_ref, y_ref, hr_ref, hi_ref,
              h_scr, c_scr):
    tb = pl.program_id(2)
    nrow = u_ref.shape[1]
    ns = SSM_SLAB

    @pl.when(tb == 0)
    def _():
        c_scr[0:1, :] = h0r_ref[0]
        c_scr[1:2, :] = h0i_ref[0]

    u = u_ref[0]
    h_scr[...] = _dot(u.astype(BF16), bw_ref[0])
    t1r, t1i, t2r, t2i, t4r, t4i, tcr, tci = (pw_ref[0, n] for n in range(8))

    def body(r, carry):
        cr, ci = carry
        rows = pl.ds(pl.multiple_of(r * SUBLANES, SUBLANES), SUBLANES)
        xr, xi = h_scr[rows, 0:ns], h_scr[rows, ns:2 * ns]
        for k, pr, pi in ((1, t1r, t1i), (2, t2r, t2i), (4, t4r, t4i)):
            sr, si = pltpu.roll(xr, k, 0), pltpu.roll(xi, k, 0)
            xr, xi = xr + pr * sr - pi * si, xi + pr * si + pi * sr
        xr, xi = xr + tcr * cr - tci * ci, xi + tcr * ci + tci * cr
        h_scr[rows, 0:ns] = xr
        h_scr[rows, ns:2 * ns] = xi
        return xr[SUBLANES - 1:SUBLANES], xi[SUBLANES - 1:SUBLANES]

    cr, ci = lax.fori_loop(0, nrow // SUBLANES, body, (c_scr[0:1, :], c_scr[1:2, :]))
    c_scr[0:1, :] = cr
    c_scr[1:2, :] = ci
    y_ref[0] = _dot(h_scr[...].astype(BF16), cw_ref[0]) + dsk_ref[...] * u

    @pl.when(tb == pl.num_programs(2) - 1)
    def _():
        hr_ref[0] = cr
        hi_ref[0] = ci


def _ssm(u3, h0r, h0i, bw, cw, pw, d_skip, tb):
    b, s, sw = u3.shape
    ns = sw // LANES
    tb = min(tb, s)
    assert s % tb == 0 and tb % SUBLANES == 0
    nstate = ns * SSM_SLAB
    slab = lambda r, c: pl.BlockSpec((1, r, c), lambda bi, g, t: (g, 0, 0))
    state = pl.BlockSpec((1, 1, SSM_SLAB), lambda bi, g, t: (bi, 0, g))
    seq = pl.BlockSpec((1, tb, LANES), lambda bi, g, t: (bi, t, g))
    return pl.pallas_call(
        _ssm_body,
        grid=(b, ns, s // tb),
        in_specs=[seq, slab(LANES, 2 * SSM_SLAB), slab(2 * SSM_SLAB, LANES),
                  pl.BlockSpec((1, 8, SUBLANES, SSM_SLAB), lambda bi, g, t: (g, 0, 0, 0)),
                  pl.BlockSpec((1, LANES), lambda bi, g, t: (0, g)), state, state],
        out_specs=[seq, state, state],
        out_shape=[jax.ShapeDtypeStruct((b, s, sw), F32), jax.ShapeDtypeStruct((b, 1, nstate), F32),
                   jax.ShapeDtypeStruct((b, 1, nstate), F32)],
        scratch_shapes=[pltpu.VMEM((tb, 2 * SSM_SLAB), F32), pltpu.VMEM((SUBLANES, SSM_SLAB), F32)],
        compiler_params=_params("parallel", "parallel", "arbitrary"),
        name="ssm",
    )(u3, bw, cw, pw, d_skip, h0r.reshape(b, 1, nstate), h0i.reshape(b, 1, nstate))


def _post_body(y_ref, o_ref, x_ref, wglu_ref, bglu_ref, gssm_ref, wout_ref, gffn_ref, wq_ref,
               h_ref, xq_ref, qp_ref):
    aw = o_ref.shape[-1]
    z = _gelu(y_ref[...])
    z = z * jax.nn.sigmoid(_dot(z.astype(BF16), wglu_ref[...]) + bglu_ref[...])
    ys = z * lax.rsqrt(jnp.mean(z * z, axis=-1, keepdims=True) + NORM_EPS) * gssm_ref[...]
    h = (x_ref[...] + _dot(o_ref[...].astype(BF16), wout_ref[0:aw, :])
         + _dot(ys.astype(BF16), wout_ref[aw:, :]))
    h_ref[...] = h
    xq = (h * lax.rsqrt(jnp.mean(h * h, axis=-1, keepdims=True) + NORM_EPS) * gffn_ref[...]).astype(BF16)
    xq_ref[...] = xq
    qp_ref[...] = _dot(xq, wq_ref[...]).astype(BF16)


def _post(y, o, x2, w_glu_b, b_glu, g_ssm, w_out_b, g_ffn, w_query_b, tb):
    t, d = x2.shape
    aw, sw = o.shape[1], y.shape[1]
    nq = w_query_b.shape[1]
    tb = min(tb, t)
    row = lambda w: pl.BlockSpec((tb, w), lambda i: (i, 0))
    return pl.pallas_call(
        _post_body,
        grid=(t // tb,),
        in_specs=[row(sw), row(aw), row(d), _const_spec((sw, sw)), _const_spec((1, sw)), _const_spec((1, sw)),
                  _const_spec((d, d)), _const_spec((1, d)), _const_spec((d, nq))],
        out_specs=[row(d), row(d), row(nq)],
        out_shape=[jax.ShapeDtypeStruct((t, d), F32), jax.ShapeDtypeStruct((t, d), BF16),
                   jax.ShapeDtypeStruct((t, nq), BF16)],
        compiler_params=_params("parallel"),
        name="post",
    )(y, o, x2, w_glu_b, b_glu, g_ssm, w_out_b, g_ffn, w_query_b)


def _top_rows(x, k):
    nrows = x.shape[0]
    rid = lax.broadcasted_iota(jnp.int32, x.shape, 0)
    vals, idx = [], []
    for _ in range(k):
        m = jnp.max(x, axis=0, keepdims=True)
        first = jnp.min(jnp.where(x == m, rid, nrows), axis=0, keepdims=True)
        x = jnp.where(rid == first, -jnp.inf, x)
        vals.append(m)
        idx.append(first)
    return vals, idx


def _peer_select_body(qp_ref, sk_ref, e0_ref, d_ref, s1_ref, e1_ref):
    nk = PEER_NKEYS
    s0 = lax.dot_general(sk_ref[0, 0], qp_ref[:, 0:nk], _NT, preferred_element_type=F32)
    s1 = lax.dot_general(sk_ref[0, 1], qp_ref[:, nk:2 * nk], _NT, preferred_element_type=F32)
    a, a_idx = _top_rows(s0, PEER_TOPK)
    b, _ = _top_rows(s1, PEER_TOPK)
    width = [PEER_TOPK // (r + 1) for r in range(PEER_TOPK)]
    cand = [[a[r] + b[c] for c in range(width[r])] for r in range(PEER_TOPK)]
    flat = [x for row in cand for x in row]
    pad = -len(flat) % SUBLANES
    f, _ = _top_rows(jnp.concatenate(flat + [jnp.full_like(a[0], -jnp.inf)] * pad, axis=0), PEER_TOPK)
    tau = f[PEER_TOPK - 1]
    z = jnp.exp(f[0] - f[0])
    for fv in f[1:]:
        z = z + jnp.exp(fv - f[0])
    rid = lax.broadcasted_iota(jnp.int32, s0.shape, 0)
    d = jnp.full(s0.shape, jnp.inf, F32)
    for r in range(PEER_TOPK):
        d_r = jnp.full_like(tau, jnp.inf)
        for c in range(width[r]):
            d_r = jnp.where(cand[r][c] >= tau, b[c], d_r)
        d = jnp.where(rid == a_idx[r], d_r, d)
    d_ref[0] = d
    s1_ref[0] = s1
    e0_ref[0] = jnp.exp(s0 - a[0])
    e1_ref[0] = jnp.exp(s1 - b[0]) / z


def _peer_select(qp, sub_keys_b, tb):
    t = qp.shape[0]
    nk = PEER_NKEYS
    tb = min(tb, t)
    tile = pl.BlockSpec((1, nk, tb), lambda i, h: (h, 0, i))
    full = jax.ShapeDtypeStruct((PEER_HEADS, nk, t), F32)
    return pl.pallas_call(
        _peer_select_body,
        grid=(t // tb, PEER_HEADS),
        in_specs=[pl.BlockSpec((tb, 2 * nk), lambda i, h: (i, h)),
                  pl.BlockSpec((1, 2, nk, nk), lambda i, h: (h, 0, 0, 0))],
        out_specs=[tile, tile, tile, tile],
        out_shape=[full, full, full, full],
        compiler_params=_params("parallel", "parallel"),
        name="peer_select",
    )(qp, sub_keys_b)


def _peer_body(xq_ref, e0_ref, d_ref, s1_ref, e1_ref, u_ref, vt_ref, h_ref, y_ref,
               acc_scr, wt_scr, *, nblocks, pieces):
    e = pl.program_id(1)
    nk = PEER_NKEYS
    eb, d = u_ref.shape
    tb = xq_ref.shape[0]
    per_block = eb // nk
    rows_a, rows_c = eb // pieces, d // pieces

    @pl.when(e == 0)
    def _():
        acc_scr[...] = jnp.zeros(acc_scr.shape, F32)
        wt_scr[...] = jnp.zeros(wt_scr.shape, BF16)

    blk = jnp.minimum(e, nblocks - 1)
    first_keys = pl.ds(pl.multiple_of(blk * per_block, per_block), per_block)

    def step(cur, prev):
        d_rows = [d_ref[h, first_keys, :] for h in range(PEER_HEADS)]
        e0_rows = [e0_ref[h, first_keys, :] for h in range(PEER_HEADS)]
        tw = min(MXU_DIM, tb)
        for t0 in range(0, tb, tw):
            toks = slice(t0, t0 + tw)
            for p in range(pieces):
                hid = lax.dot_general(u_ref[p * rows_a:(p + 1) * rows_a, :], xq_ref[toks, :], _NT,
                                      preferred_element_type=F32)
                slab = slice(p * rows_c, (p + 1) * rows_c)
                acc_scr[slab, toks] += _dot(vt_ref[slab, :], wt_scr[prev, :, toks])
                for il in range(rows_a // nk):
                    i = p * (rows_a // nk) + il
                    for c in range(0, tw, LANES):
                        lanes = slice(t0 + c, t0 + c + LANES)
                        for j0 in range(0, nk, GATE_ROWS):
                            keys = slice(j0, j0 + GATE_ROWS)
                            gate = jnp.zeros((GATE_ROWS, LANES), F32)
                            for h in range(PEER_HEADS):
                                sel = s1_ref[h, keys, lanes] >= d_rows[h][i:i + 1, lanes]
                                gate = gate + jnp.where(sel, e0_rows[h][i:i + 1, lanes] * e1_ref[h, keys, lanes], 0.0)
                            act = _gelu(hid[il * nk + j0:il * nk + j0 + GATE_ROWS, c:c + LANES])
                            rows = slice(p * rows_a + il * nk + j0, p * rows_a + il * nk + j0 + GATE_ROWS)
                            wt_scr[cur, rows, lanes] = (gate * act).astype(BF16)

    for parity in (0, 1):
        pl.when(e % 2 == parity)(functools.partial(step, parity, 1 - parity))

    @pl.when(e == pl.num_programs(1) - 1)
    def _():
        y_ref[...] = h_ref[...] + acc_scr[...].T


def _peer(xq, sel, u_b, vt_b, h, tb, eb):
    t, d = xq.shape
    ne = u_b.shape[0]
    nk = PEER_NKEYS
    tb = min(tb, t)
    nblocks = ne // eb
    single = pl.Buffered(1)
    tile = pl.BlockSpec((PEER_HEADS, nk, tb), lambda i, e: (0, 0, i), pipeline_mode=single)
    return pl.pallas_call(
        functools.partial(_peer_body, nblocks=nblocks, pieces=4),
        grid=(t // tb, nblocks + 1),
        in_specs=[pl.BlockSpec((tb, d), lambda i, e: (i, 0), pipeline_mode=single),
                  tile, tile, tile, tile,
                  pl.BlockSpec((eb, d), lambda i, e: (jnp.minimum(e, nblocks - 1), 0)),
                  pl.BlockSpec((d, eb), lambda i, e: (0, jnp.maximum(e - 1, 0))),
                  pl.BlockSpec((tb, d), lambda i, e: (i, 0), pipeline_mode=single)],
        out_specs=pl.BlockSpec((tb, d), lambda i, e: (i, 0)),
        out_shape=jax.ShapeDtypeStruct((t, d), F32),
        scratch_shapes=[pltpu.VMEM((d, tb), F32), pltpu.VMEM((2, eb, tb), BF16)],
        compiler_params=_params("parallel", "arbitrary"),
        name="peer",
    )(xq, *sel, u_b, vt_b, h)


def _layer(x, k_past, v_past, h0_re, h0_im, lambda_init, w):
    b, s, d = x.shape
    t = b * s
    aw = ATT_HEADS * V_DIM
    x2 = x.reshape(t, d)
    qb, k, kb, v, vb, u = _in_proj(x2, w['g_mix'], w['w_in'], w['seg'], w['gq'], w['gk'], tb=256)
    if k_past is None:
        assert b == 1
        o = _attn_prompt(w['bounded'], qb, kb, vb, w['lam'], w['g_sub'], lambda_init, tq=512, tk=1024, rt=256)
        h0_re = jnp.zeros((b, w['nstate']), F32)
        h0_im = h0_re
        ssm_tb = 512
    else:
        past = k_past.shape[1]
        r3 = lambda a: a.reshape(b, s, aw)
        k_t = jnp.transpose(k_past, (0, 2, 3, 4, 1)).reshape(b, ATT_HEADS, 2 * QK_DIM, past)
        o = _attn_sample(r3(qb), k_t, v_past.reshape(b, past * ATT_HEADS, V_DIM), r3(kb),
                         vb.reshape(b, s, 2 * aw), w['lam'], w['g_sub'], lambda_init).reshape(t, aw)
        ssm_tb = s
    y, h_re, h_im = _ssm(u.reshape(b, s, -1), h0_re.reshape(b, -1), h0_im.reshape(b, -1),
                         w['bw'], w['cw'], w['pw'], w['d_skip'], tb=ssm_tb)
    hres, xq, qp = _post(y.reshape(t, -1), o, x2, w['w_glu'], w['b_glu'], w['g_ssm'], w['w_out'], w['g_ffn'],
                         w['w_query'], tb=256)
    sel = _peer_select(qp, w['sub_keys'], tb=512)
    out = _peer(xq, sel, w['u_table'], w['v_table_t'], hres, tb=512, eb=1024)
    ngroups = w['nstate'] // SSM_STATE
    return (out.reshape(b, s, d), k.reshape(b, s, ATT_HEADS, 2, QK_DIM), v.reshape(b, s, ATT_HEADS, V_DIM),
            h_re.reshape(b, ngroups, SSM_STATE), h_im.reshape(b, ngroups, SSM_STATE))


def _prep_weights(l, g_mix, w_in, g_q, g_k, lam_q1, lam_k1, lam_q2, lam_k2, g_sub, a_re, a_im, log_dt, b_re,
                  b_im, c_re, c_im, d_skip, w_glu, b_glu, g_ssm, w_out, g_ffn, w_query, sub_keys, u_table,
                  v_table):
    aw = ATT_HEADS * V_DIM
    row = lambda a: a[l].reshape(1, -1)
    seg_id = jnp.arange(aw) // QK_DIM
    seg = jnp.where(seg_id[:, None] == seg_id[None, :], 1.0 / QK_DIM, 0.0).astype(BF16)
    bw, cw, pw = _ssm_params(a_re[l], a_im[l], log_dt[l], b_re[l], b_im[l], c_re[l], c_im[l])
    q_scale = QK_DIM ** -0.5 * LOG2_E
    score_bound = QK_DIM * q_scale * jnp.max(jnp.abs(g_q[l])) * jnp.max(jnp.abs(g_k[l])) * BF16_SLACK
    return dict(
        bounded=(score_bound <= MAX_UNSHIFTED_LOG2).astype(jnp.int32).reshape(1),
        g_mix=row(g_mix), w_in=w_in[l].astype(BF16), seg=seg,
        gq=jnp.tile(g_q[l], aw // QK_DIM).reshape(1, aw) * q_scale,
        gk=jnp.tile(g_k[l], aw // QK_DIM).reshape(1, aw),
        lam=(row(lam_q1), row(lam_k1), row(lam_q2), row(lam_k2)), g_sub=row(g_sub),
        bw=bw, cw=cw, pw=pw, d_skip=row(d_skip), nstate=a_re.shape[1] * a_re.shape[2],
        w_glu=w_glu[l].astype(BF16), b_glu=row(b_glu), g_ssm=row(g_ssm), w_out=w_out[l].astype(BF16),
        g_ffn=row(g_ffn), w_query=w_query[l].astype(BF16), sub_keys=sub_keys[l].astype(BF16),
        u_table=u_table[l].astype(BF16), v_table_t=v_table[l].T.astype(BF16))


def kernel(x_prompt, x_sample, cache_k, cache_v, state_ssm_re, state_ssm_im, g_mix, w_in, g_q, g_k, lam_q1, lam_k1, lam_q2, lam_k2, g_sub, a_re, a_im, log_dt, b_re, b_im, c_re, c_im, d_skip, w_glu, b_glu, g_ssm, w_out, g_ffn, w_query, sub_keys, u_table, v_table):
    depth = w_in.shape[0]
    yp, ys = x_prompt, x_sample
    outs = [[] for _ in range(8)]
    for l in range(depth):
        lambda_init = 0.8 - 0.6 * math.exp(-0.3 * l)
        w = _prep_weights(l, g_mix, w_in, g_q, g_k, lam_q1, lam_k1, lam_q2, lam_k2, g_sub, a_re, a_im, log_dt,
                          b_re, b_im, c_re, c_im, d_skip, w_glu, b_glu, g_ssm, w_out, g_ffn, w_query, sub_keys,
                          u_table, v_table)
        yp, kp, vp, hrp, hip = _layer(yp, None, None, None, None, lambda_init, w)
        ys, kk, vv, hrs, his = _layer(ys, cache_k[l], cache_v[l], state_ssm_re[l], state_ssm_im[l],
                                      lambda_init, w)
        for acc, val in zip(outs, (kp, vp, hrp, hip, kk, vv, hrs, his)):
            acc.append(val)
    return (yp, ys) + tuple(jnp.stack(o) for o in outs)
```

```python
import functools
import math

import jax
import jax.numpy as jnp
from jax import lax
from jax.experimental import pallas as pl
from jax.experimental.pallas import tpu as pltpu

F32 = jnp.float32
BF16 = jnp.bfloat16

CHUNK = 64
ATT_HEADS = 8
QK_DIM = 64
V_DIM = 2 * QK_DIM
SSM_GROUP = 16
SSM_STATE = 64
PEER_HEADS = 8
PEER_NKEYS = 128
PEER_TOPK = 16
NORM_EPS = 1e-6

LANES = 128
SUBLANES = 8
MXU_DIM = 256
GATE_ROWS = 32
SSM_SUPER = LANES // SSM_GROUP
SSM_SLAB = SSM_SUPER * SSM_STATE
VMEM_LIMIT = 56 * 1024 * 1024
SQRT_HALF = math.sqrt(0.5)
LOG2_E = math.log2(math.e)
BF16_SLACK = 1.0 + 2.0 ** -6
MAX_UNSHIFTED_LOG2 = 60.0

_NT = (((1,), (1,)), ((), ()))


def _params(*sem):
    return pltpu.CompilerParams(dimension_semantics=sem, vmem_limit_bytes=VMEM_LIMIT)


def _const_spec(shape):
    nd = len(shape)
    return pl.BlockSpec(shape, lambda *_: (0,) * nd, pipeline_mode=pl.Buffered(1))


def _gelu(x):
    return 0.5 * x * (1.0 + lax.erf(x * SQRT_HALF))


def _dot(a, b):
    return jnp.dot(a, b, preferred_element_type=F32)


def _in_proj_body(x_ref, gmix_ref, w_ref, seg_ref, gq_ref, gk_ref,
                  qb_ref, k_ref, kb_ref, v_ref, vb_ref, u_ref):
    aw = k_ref.shape[-1]
    x = x_ref[...]
    ms = jnp.mean(x * x, axis=-1, keepdims=True)
    xn = (x * lax.rsqrt(ms + NORM_EPS) * gmix_ref[...]).astype(BF16)

    def seg_rms(y, g):
        sq = y * y
        hi = sq.astype(BF16)
        lo = (sq - hi.astype(F32)).astype(BF16)
        mean = _dot(hi, seg_ref[...]) + _dot(lo, seg_ref[...])
        return y * lax.rsqrt(mean + NORM_EPS) * g

    q = seg_rms(_dot(xn, w_ref[:, 0:aw]), gq_ref[...])
    qb_ref[...] = q.astype(BF16)
    k = seg_rms(_dot(xn, w_ref[:, aw:2 * aw]), gk_ref[...])
    k_ref[...] = k
    kb_ref[...] = k.astype(BF16)
    v = _dot(xn, w_ref[:, 2 * aw:3 * aw])
    v_ref[...] = v
    vb = v.astype(BF16)
    ones = jnp.ones((vb.shape[0], V_DIM), BF16)
    for h in range(aw // V_DIM):
        vb_ref[:, 2 * h * V_DIM:(2 * h + 1) * V_DIM] = vb[:, h * V_DIM:(h + 1) * V_DIM]
        vb_ref[:, (2 * h + 1) * V_DIM:(2 * h + 2) * V_DIM] = ones
    u_ref[...] = _dot(xn, w_ref[:, 3 * aw:])


def _in_proj(x2, g_mix, w_in_b, seg, gq_t, gk_t, tb):
    t, d = x2.shape
    n = w_in_b.shape[1]
    aw = seg.shape[0]
    sw = n - 3 * aw
    tb = min(tb, t)
    row = lambda w: pl.BlockSpec((tb, w), lambda i: (i, 0))
    return pl.pallas_call(
        _in_proj_body,
        grid=(t // tb,),
        in_specs=[row(d), _const_spec((1, d)), _const_spec((d, n)), _const_spec((aw, aw)),
                  _const_spec((1, aw)), _const_spec((1, aw))],
        out_specs=[row(aw), row(aw), row(aw), row(aw), row(2 * aw), row(sw)],
        out_shape=[jax.ShapeDtypeStruct((t, aw), BF16), jax.ShapeDtypeStruct((t, aw), F32),
                   jax.ShapeDtypeStruct((t, aw), BF16), jax.ShapeDtypeStruct((t, aw), F32),
                   jax.ShapeDtypeStruct((t, 2 * aw), BF16), jax.ShapeDtypeStruct((t, sw), F32)],
        compiler_params=_params("parallel"),
        name="in_proj",
    )(x2, g_mix, w_in_b, seg, gq_t, gk_t)


def _lambda(lq1_ref, lk1_ref, lq2_ref, lk2_ref, lambda_init):
    s1 = jnp.sum(lq1_ref[...] * lk1_ref[...], axis=-1, keepdims=True)
    s2 = jnp.sum(lq2_ref[...] * lk2_ref[...], axis=-1, keepdims=True)
    return jnp.exp(s1) - jnp.exp(s2) + lambda_init


def _stack_maps(q):
    lane = lax.broadcasted_iota(jnp.int32, q.shape, 1)
    zero = jnp.zeros_like(q)
    return jnp.concatenate([jnp.where(lane < QK_DIM, q, zero), jnp.where(lane >= QK_DIM, q, zero)], axis=0)


def _sub_norm(o2, l, lam, gsub, nq, lambda_init):
    o2 = o2 / l
    o = o2[:nq] - lam * o2[nq:]
    ms = jnp.mean(o * o, axis=-1, keepdims=True)
    return (o * lax.rsqrt(ms + NORM_EPS) * gsub) * (1.0 - lambda_init)


def _attn_prompt_body(bounded_ref, lq1_ref, lk1_ref, lq2_ref, lk2_ref, gsub_ref, q_ref, k_ref, v_ref, o_ref,
                      q2_scr, m_scr, acc_scr, *, tq, tk, rt, lambda_init):
    i = pl.program_id(1)
    q2_scr[...] = _stack_maps(q_ref[...])
    acc_scr[...] = jnp.zeros(acc_scr.shape, F32)
    jd = (i * tq) // tk

    def scores(key0, nkeys, r0, masked):
        keys = pl.ds(pl.multiple_of(key0, tq), nkeys)
        s = lax.dot_general(q2_scr[r0:r0 + rt, :], k_ref[keys, :], _NT, preferred_element_type=F32)
        if masked:
            row = lax.broadcasted_iota(jnp.int32, s.shape, 0)
            col = lax.broadcasted_iota(jnp.int32, s.shape, 1)
            q_chunk = (i * tq + (r0 % tq) + row) // CHUNK
            k_chunk = (key0 + col) // CHUNK
            s = jnp.where(k_chunk <= q_chunk, s, -jnp.inf)
        return s, keys

    def unshifted_step(key0, nkeys, masked):
        for r0 in range(0, 2 * tq, rt):
            s, keys = scores(key0, nkeys, r0, masked)
            acc_scr[r0:r0 + rt, :] += _dot(jnp.exp2(s).astype(BF16), v_ref[keys, :])

    def online_step(key0, nkeys, masked):
        for r0 in range(0, 2 * tq, rt):
            rows = slice(r0, r0 + rt)
            s, keys = scores(key0, nkeys, r0, masked)
            m_prev = m_scr[rows, :]
            m_new = jnp.maximum(m_prev, jnp.max(s, axis=1, keepdims=True))
            p = jnp.exp2(s - m_new)
            acc_scr[rows, :] = jnp.exp2(m_prev - m_new) * acc_scr[rows, :] + _dot(p.astype(BF16), v_ref[keys, :])
            m_scr[rows, :] = m_new

    def sweep(step):
        def full_step(j, c):
            step(j * tk, tk, False)
            return c
        lax.fori_loop(0, jd, full_step, 0)
        if tk > tq:
            pl.when(i * tq > jd * tk)(lambda: step(jd * tk, tq, False))
        step(i * tq, tq, True)

    @pl.when(bounded_ref[0] == 1)
    def _():
        sweep(unshifted_step)

    @pl.when(bounded_ref[0] != 1)
    def _():
        m_scr[...] = jnp.full(m_scr.shape, -jnp.inf, F32)
        sweep(online_step)

    lam = _lambda(lq1_ref, lk1_ref, lq2_ref, lk2_ref, lambda_init)
    acc = acc_scr[...]
    o_ref[...] = _sub_norm(acc[:, :V_DIM], acc[:, V_DIM:V_DIM + 1], lam, gsub_ref[...], tq, lambda_init)


def _attn_prompt(bounded, qb, kb, vb1, lam_parts, g_sub, lambda_init, tq, tk, rt):
    s, aw = qb.shape
    tk = min(tk, s)
    tq = min(tq, tk)
    rt = min(rt, tq)
    assert s % tk == 0 and tk in (tq, 2 * tq) and tq % rt == 0 and rt % CHUNK == 0
    lam_specs = [_const_spec((1, QK_DIM))] * 4
    return pl.pallas_call(
        functools.partial(_attn_prompt_body, tq=tq, tk=tk, rt=rt, lambda_init=lambda_init),
        grid=(ATT_HEADS, s // tq),
        in_specs=[pl.BlockSpec(memory_space=pltpu.SMEM)] + lam_specs + [
            _const_spec((1, V_DIM)), pl.BlockSpec((tq, V_DIM), lambda h, i: (i, h)),
            pl.BlockSpec((s, V_DIM), lambda h, i: (0, h)), pl.BlockSpec((s, 2 * V_DIM), lambda h, i: (0, h))],
        out_specs=pl.BlockSpec((tq, V_DIM), lambda h, i: (i, h)),
        out_shape=jax.ShapeDtypeStruct((s, aw), F32),
        scratch_shapes=[pltpu.VMEM((2 * tq, V_DIM), BF16), pltpu.VMEM((2 * tq, 1), F32),
                        pltpu.VMEM((2 * tq, 2 * V_DIM), F32)],
        compiler_params=_params("parallel", "arbitrary"),
        name="attn_prompt",
    )(bounded, *lam_parts, g_sub, qb, kb, vb1)


def _attn_sample_body(lq1_ref, lk1_ref, lq2_ref, lk2_ref, gsub_ref, q_ref, kp_ref, vp_ref, kn_ref, vn_ref,
                      o_ref, *, past, lambda_init):
    nq = q_ref.shape[1]
    q2 = _stack_maps(q_ref[0])
    s_past = _dot(q2, kp_ref[0, 0].astype(BF16))
    v_past = vp_ref[0, pl.ds(pl.program_id(1), past, stride=ATT_HEADS), :]
    s_new = lax.dot_general(q2, kn_ref[0], _NT, preferred_element_type=F32)
    row = lax.broadcasted_iota(jnp.int32, s_new.shape, 0)
    col = lax.broadcasted_iota(jnp.int32, s_new.shape, 1)
    q_chunk = (past + jnp.where(row >= nq, row - nq, row)) // CHUNK
    s_new = jnp.where((past + col) // CHUNK <= q_chunk, s_new, -jnp.inf)
    m = jnp.maximum(jnp.max(s_past, axis=1, keepdims=True), jnp.max(s_new, axis=1, keepdims=True))
    p_past = jnp.exp2(s_past - m)
    p_new = jnp.exp2(s_new - m)
    l = jnp.sum(p_past, axis=1, keepdims=True) + jnp.sum(p_new, axis=1, keepdims=True)
    acc = _dot(p_past.astype(BF16), v_past.astype(BF16)) + _dot(p_new.astype(BF16), vn_ref[0])
    lam = _lambda(lq1_ref, lk1_ref, lq2_ref, lk2_ref, lambda_init)
    o_ref[0] = _sub_norm(acc, l, lam, gsub_ref[...], nq, lambda_init)


def _attn_sample(qb3, kpt, vp3, kb3, vb3, lam_parts, g_sub, lambda_init):
    b, s, aw = qb3.shape
    past = kpt.shape[-1]
    assert past % CHUNK == 0
    lam_specs = [_const_spec((1, QK_DIM))] * 4
    new = pl.BlockSpec((1, s, V_DIM), lambda bi, h: (bi, 0, h))
    new_v = pl.BlockSpec((1, s, V_DIM), lambda bi, h: (bi, 0, 2 * h))
    old_k = pl.BlockSpec((1, 1, 2 * QK_DIM, past), lambda bi, h: (bi, h, 0, 0))
    old_v = pl.BlockSpec((1, past * ATT_HEADS, V_DIM), lambda bi, h: (bi, 0, 0))
    return pl.pallas_call(
        functools.partial(_attn_sample_body, past=past, lambda_init=lambda_init),
        grid=(b, ATT_HEADS),
        in_specs=lam_specs + [_const_spec((1, V_DIM)), new, old_k, old_v, new, new_v],
        out_specs=new,
        out_shape=jax.ShapeDtypeStruct((b, s, aw), F32),
        compiler_params=_params("parallel", "arbitrary"),
        name="attn_sample",
    )(*lam_parts, g_sub, qb3, kpt, vp3, kb3, vb3)


def _ssm_param_body(are_ref, aim_ref, ldt_ref, bre_ref, bim_ref, abr_ref, abi_ref, bbr_ref, bbi_ref):
    ar, ai = are_ref[...], aim_ref[...]
    dt = jnp.exp(ldt_ref[...])
    mag = jnp.exp(ar * dt)
    abr, abi = mag * jnp.cos(ai * dt), mag * jnp.sin(ai * dt)
    den = ar * ar + ai * ai
    nr, ni = abr - 1.0, abi
    fr, fi = (nr * ar + ni * ai) / den, (ni * ar - nr * ai) / den
    br, bi = bre_ref[...], bim_ref[...]
    abr_ref[...] = abr
    abi_ref[...] = abi
    bbr_ref[...] = fr * br - fi * bi
    bbi_ref[...] = fr * bi + fi * br


def _ssm_params(a_re, a_im, log_dt, b_re, b_im, c_re, c_im):
    g, p = a_re.shape
    c = b_re.shape[-1]
    n = g * p
    col = lambda a: a.reshape(n, 1)
    abr, abi, bbr, bbi = pl.pallas_call(
        _ssm_param_body,
        out_shape=[jax.ShapeDtypeStruct((n, 1), F32)] * 2 + [jax.ShapeDtypeStruct((n, c), F32)] * 2,
        name="ssm_params",
    )(col(a_re), col(a_im), col(jnp.repeat(log_dt, p)), b_re.reshape(n, c), b_im.reshape(n, c))
    ns = g // SSM_SUPER
    eye = jnp.eye(SSM_SUPER, dtype=F32)

    def in_blocks(bb):
        bb = bb.reshape(ns, SSM_SUPER, p, c)
        return jnp.einsum('sgpc,gh->sgchp', bb, eye).reshape(ns, SSM_SUPER * c, SSM_SLAB)

    def out_blocks(cc):
        cc = cc.reshape(ns, SSM_SUPER, c, p)
        return jnp.einsum('sgcp,gh->sgphc', cc, eye).reshape(ns, SSM_SLAB, SSM_SUPER * c)

    bw = jnp.concatenate([in_blocks(bbr), in_blocks(bbi)], axis=2).astype(BF16)
    cw = jnp.concatenate([out_blocks(c_re), -out_blocks(c_im)], axis=1).astype(BF16)
    pr, pi = [abr.reshape(ns, 1, SSM_SLAB)], [abi.reshape(ns, 1, SSM_SLAB)]
    for _ in range(SUBLANES - 1):
        pr, pi = (pr + [pr[-1] * pr[0] - pi[-1] * pi[0]], pi + [pr[-1] * pi[0] + pi[-1] * pr[0]])
    rows = jnp.arange(SUBLANES).reshape(1, SUBLANES, 1)
    tabs = []
    for k in (1, 2, 4):
        tabs += [jnp.where(rows >= k, pr[k - 1], 0.0), jnp.where(rows >= k, pi[k - 1], 0.0)]
    tabs += [jnp.concatenate(pr, axis=1), jnp.concatenate(pi, axis=1)]
    return bw, cw, jnp.stack(tabs, axis=1)


def _ssm_body(u_ref, bw_ref, cw_ref, pw_ref, dsk_ref, h0r_ref, h0i_ref, y_ref, hr_ref, hi_ref,
              h_scr, c_scr):
    tb = pl.program_id(2)
    nrow = u_ref.shape[1]
    ns = SSM_SLAB

    @pl.when(tb == 0)
    def _():
        c_scr[0:1, :] = h0r_ref[0]
        c_scr[1:2, :] = h0i_ref[0]

    u = u_ref[0]
    h_scr[...] = _dot(u.astype(BF16), bw_ref[0])
    t1r, t1i, t2r, t2i, t4r, t4i, tcr, tci = (pw_ref[0, n] for n in range(8))

    def body(r, carry):
        cr, ci = carry
        rows = pl.ds(pl.multiple_of(r * SUBLANES, SUBLANES), SUBLANES)
        xr, xi = h_scr[rows, 0:ns], h_scr[rows, ns:2 * ns]
        for k, pr, pi in ((1, t1r, t1i), (2, t2r, t2i), (4, t4r, t4i)):
            sr, si = pltpu.roll(xr, k, 0), pltpu.roll(xi, k, 0)
            xr, xi = xr + pr * sr - pi * si, xi + pr * si + pi * sr
        xr, xi = xr + tcr * cr - tci * ci, xi + tcr * ci + tci * cr
        h_scr[rows, 0:ns] = xr
        h_scr[rows, ns:2 * ns] = xi
        return xr[SUBLANES - 1:SUBLANES], xi[SUBLANES - 1:SUBLANES]

    cr, ci = lax.fori_loop(0, nrow // SUBLANES, body, (c_scr[0:1, :], c_scr[1:2, :]), unroll=2)
    c_scr[0:1, :] = cr
    c_scr[1:2, :] = ci
    y_ref[0] = _dot(h_scr[...].astype(BF16), cw_ref[0]) + dsk_ref[...] * u

    @pl.when(tb == pl.num_programs(2) - 1)
    def _():
        hr_ref[0] = cr
        hi_ref[0] = ci


def _ssm(u3, h0r, h0i, bw, cw, pw, d_skip, tb):
    b, s, sw = u3.shape
    ns = sw // LANES
    tb = min(tb, s)
    assert s % tb == 0 and tb % SUBLANES == 0
    nstate = ns * SSM_SLAB
    slab = lambda r, c: pl.BlockSpec((1, r, c), lambda bi, g, t: (g, 0, 0))
    state = pl.BlockSpec((1, 1, SSM_SLAB), lambda bi, g, t: (bi, 0, g))
    seq = pl.BlockSpec((1, tb, LANES), lambda bi, g, t: (bi, t, g))
    return pl.pallas_call(
        _ssm_body,
        grid=(b, ns, s // tb),
        in_specs=[seq, slab(LANES, 2 * SSM_SLAB), slab(2 * SSM_SLAB, LANES),
                  pl.BlockSpec((1, 8, SUBLANES, SSM_SLAB), lambda bi, g, t: (g, 0, 0, 0)),
                  pl.BlockSpec((1, LANES), lambda bi, g, t: (0, g)), state, state],
        out_specs=[seq, state, state],
        out_shape=[jax.ShapeDtypeStruct((b, s, sw), F32), jax.ShapeDtypeStruct((b, 1, nstate), F32),
                   jax.ShapeDtypeStruct((b, 1, nstate), F32)],
        scratch_shapes=[pltpu.VMEM((tb, 2 * SSM_SLAB), F32), pltpu.VMEM((SUBLANES, SSM_SLAB), F32)],
        compiler_params=_params("parallel", "parallel", "arbitrary"),
        name="ssm",
    )(u3, bw, cw, pw, d_skip, h0r.reshape(b, 1, nstate), h0i.reshape(b, 1, nstate))


def _post_body(y_ref, o_ref, x_ref, wglu_ref, bglu_ref, gssm_ref, wout_ref, gffn_ref, wq_ref,
               h_ref, xq_ref, qp_ref):
    aw = o_ref.shape[-1]
    z = _gelu(y_ref[...])
    z = z * jax.nn.sigmoid(_dot(z.astype(BF16), wglu_ref[...]) + bglu_ref[...])
    ys = z * lax.rsqrt(jnp.mean(z * z, axis=-1, keepdims=True) + NORM_EPS) * gssm_ref[...]
    h = (x_ref[...] + _dot(o_ref[...].astype(BF16), wout_ref[0:aw, :])
         + _dot(ys.astype(BF16), wout_ref[aw:, :]))
    h_ref[...] = h
    xq = (h * lax.rsqrt(jnp.mean(h * h, axis=-1, keepdims=True) + NORM_EPS) * gffn_ref[...]).astype(BF16)
    xq_ref[...] = xq
    qp_ref[...] = _dot(xq, wq_ref[...]).astype(BF16)


def _post(y, o, x2, w_glu_b, b_glu, g_ssm, w_out_b, g_ffn, w_query_b, tb):
    t, d = x2.shape
    aw, sw = o.shape[1], y.shape[1]
    nq = w_query_b.shape[1]
    tb = min(tb, t)
    row = lambda w: pl.BlockSpec((tb, w), lambda i: (i, 0))
    return pl.pallas_call(
        _post_body,
        grid=(t // tb,),
        in_specs=[row(sw), row(aw), row(d), _const_spec((sw, sw)), _const_spec((1, sw)), _const_spec((1, sw)),
                  _const_spec((d, d)), _const_spec((1, d)), _const_spec((d, nq))],
        out_specs=[row(d), row(d), row(nq)],
        out_shape=[jax.ShapeDtypeStruct((t, d), F32), jax.ShapeDtypeStruct((t, d), BF16),
                   jax.ShapeDtypeStruct((t, nq), BF16)],
        compiler_params=_params("parallel"),
        name="post",
    )(y, o, x2, w_glu_b, b_glu, g_ssm, w_out_b, g_ffn, w_query_b)


def _top_rows(x, k):
    nrows = x.shape[0]
    rid = lax.broadcasted_iota(jnp.int32, x.shape, 0)
    vals, idx = [], []
    for _ in range(k):
        m = jnp.max(x, axis=0, keepdims=True)
        first = jnp.min(jnp.where(x == m, rid, nrows), axis=0, keepdims=True)
        x = jnp.where(rid == first, -jnp.inf, x)
        vals.append(m)
        idx.append(first)
    return vals, idx


def _peer_select_body(qp_ref, sk_ref, e0_ref, d_ref, s1_ref, e1_ref):
    nk = PEER_NKEYS
    s0 = lax.dot_general(sk_ref[0, 0], qp_ref[:, 0:nk], _NT, preferred_element_type=F32)
    s1 = lax.dot_general(sk_ref[0, 1], qp_ref[:, nk:2 * nk], _NT, preferred_element_type=F32)
    a, a_idx = _top_rows(s0, PEER_TOPK)
    b, _ = _top_rows(s1, PEER_TOPK)
    width = [PEER_TOPK // (r + 1) for r in range(PEER_TOPK)]
    cand = [[a[r] + b[c] for c in range(width[r])] for r in range(PEER_TOPK)]
    flat = [x for row in cand for x in row]
    pad = -len(flat) % SUBLANES
    f, _ = _top_rows(jnp.concatenate(flat + [jnp.full_like(a[0], -jnp.inf)] * pad, axis=0), PEER_TOPK)
    tau = f[PEER_TOPK - 1]
    z = jnp.exp(f[0] - f[0])
    for fv in f[1:]:
        z = z + jnp.exp(fv - f[0])
    rid = lax.broadcasted_iota(jnp.int32, s0.shape, 0)
    d = jnp.full(s0.shape, jnp.inf, F32)
    for r in range(PEER_TOPK):
        d_r = jnp.full_like(tau, jnp.inf)
        for c in range(width[r]):
            d_r = jnp.where(cand[r][c] >= tau, b[c], d_r)
        d = jnp.where(rid == a_idx[r], d_r, d)
    d_ref[0] = d
    s1_ref[0] = s1
    e0_ref[0] = jnp.exp(s0 - a[0])
    e1_ref[0] = jnp.exp(s1 - b[0]) / z


def _peer_select(qp, sub_keys_b, tb):
    t = qp.shape[0]
    nk = PEER_NKEYS
    tb = min(tb, t)
    tile = pl.BlockSpec((1, nk, tb), lambda i, h: (h, 0, i))
    full = jax.ShapeDtypeStruct((PEER_HEADS, nk, t), F32)
    return pl.pallas_call(
        _peer_select_body,
        grid=(t // tb, PEER_HEADS),
        in_specs=[pl.BlockSpec((tb, 2 * nk), lambda i, h: (i, h)),
                  pl.BlockSpec((1, 2, nk, nk), lambda i, h: (h, 0, 0, 0))],
        out_specs=[tile, tile, tile, tile],
        out_shape=[full, full, full, full],
        compiler_params=_params("parallel", "parallel"),
        name="peer_select",
    )(qp, sub_keys_b)


def _peer_body(xq_ref, e0_ref, d_ref, s1_ref, e1_ref, u_ref, vt_ref, h_ref, y_ref,
               acc_scr, wt_scr, *, nblocks, pieces):
    e = pl.program_id(1)
    nk = PEER_NKEYS
    eb, d = u_ref.shape
    tb = xq_ref.shape[0]
    per_block = eb // nk
    rows_a, rows_c = eb // pieces, d // pieces

    @pl.when(e == 0)
    def _():
        acc_scr[...] = jnp.zeros(acc_scr.shape, F32)
        wt_scr[...] = jnp.zeros(wt_scr.shape, BF16)

    blk = jnp.minimum(e, nblocks - 1)
    first_keys = pl.ds(pl.multiple_of(blk * per_block, per_block), per_block)

    def step(cur, prev):
        d_rows = [d_ref[h, first_keys, :] for h in range(PEER_HEADS)]
        e0_rows = [e0_ref[h, first_keys, :] for h in range(PEER_HEADS)]
        tw = min(MXU_DIM, tb)
        for t0 in range(0, tb, tw):
            toks = slice(t0, t0 + tw)
            for p in range(pieces):
                hid = lax.dot_general(u_ref[p * rows_a:(p + 1) * rows_a, :], xq_ref[toks, :], _NT,
                                      preferred_element_type=F32)
                slab = slice(p * rows_c, (p + 1) * rows_c)
                acc_scr[slab, toks] += _dot(vt_ref[slab, :], wt_scr[prev, :, toks])
                for il in range(rows_a // nk):
                    i = p * (rows_a // nk) + il
                    for c in range(0, tw, LANES):
                        lanes = slice(t0 + c, t0 + c + LANES)
                        for j0 in range(0, nk, GATE_ROWS):
                            keys = slice(j0, j0 + GATE_ROWS)
                            gate = jnp.zeros((GATE_ROWS, LANES), F32)
                            for h in range(PEER_HEADS):
                                sel = s1_ref[h, keys, lanes] >= d_rows[h][i:i + 1, lanes]
                                gate = gate + jnp.where(sel, e0_rows[h][i:i + 1, lanes] * e1_ref[h, keys, lanes], 0.0)
                            act = _gelu(hid[il * nk + j0:il * nk + j0 + GATE_ROWS, c:c + LANES])
                            rows = slice(p * rows_a + il * nk + j0, p * rows_a + il * nk + j0 + GATE_ROWS)
                            wt_scr[cur, rows, lanes] = (gate * act).astype(BF16)

    for parity in (0, 1):
        pl.when(e % 2 == parity)(functools.partial(step, parity, 1 - parity))

    @pl.when(e == pl.num_programs(1) - 1)
    def _():
        y_ref[...] = h_ref[...] + acc_scr[...].T


def _peer(xq, sel, u_b, vt_b, h, tb, eb):
    t, d = xq.shape
    ne = u_b.shape[0]
    nk = PEER_NKEYS
    tb = min(tb, t)
    nblocks = ne // eb
    single = pl.Buffered(1)
    tile = pl.BlockSpec((PEER_HEADS, nk, tb), lambda i, e: (0, 0, i), pipeline_mode=single)
    return pl.pallas_call(
        functools.partial(_peer_body, nblocks=nblocks, pieces=4),
        grid=(t // tb, nblocks + 1),
        in_specs=[pl.BlockSpec((tb, d), lambda i, e: (i, 0), pipeline_mode=single),
                  tile, tile, tile, tile,
                  pl.BlockSpec((eb, d), lambda i, e: (jnp.minimum(e, nblocks - 1), 0)),
                  pl.BlockSpec((d, eb), lambda i, e: (0, jnp.maximum(e - 1, 0))),
                  pl.BlockSpec((tb, d), lambda i, e: (i, 0), pipeline_mode=single)],
        out_specs=pl.BlockSpec((tb, d), lambda i, e: (i, 0)),
        out_shape=jax.ShapeDtypeStruct((t, d), F32),
        scratch_shapes=[pltpu.VMEM((d, tb), F32), pltpu.VMEM((2, eb, tb), BF16)],
        compiler_params=_params("parallel", "arbitrary"),
        name="peer",
    )(xq, *sel, u_b, vt_b, h)


def _layer(x, k_past, v_past, h0_re, h0_im, lambda_init, w):
    b, s, d = x.shape
    t = b * s
    aw = ATT_HEADS * V_DIM
    x2 = x.reshape(t, d)
    qb, k, kb, v, vb, u = _in_proj(x2, w['g_mix'], w['w_in'], w['seg'], w['gq'], w['gk'], tb=256)
    if k_past is None:
        assert b == 1
        o = _attn_prompt(w['bounded'], qb, kb, vb, w['lam'], w['g_sub'], lambda_init, tq=1024, tk=2048, rt=256)
        h0_re = jnp.zeros((b, w['nstate']), F32)
        h0_im = h0_re
        ssm_tb = 512
    else:
        past = k_past.shape[1]
        r3 = lambda a: a.reshape(b, s, aw)
        k_t = jnp.transpose(k_past, (0, 2, 3, 4, 1)).reshape(b, ATT_HEADS, 2 * QK_DIM, past)
        o = _attn_sample(r3(qb), k_t, v_past.reshape(b, past * ATT_HEADS, V_DIM), r3(kb),
                         vb.reshape(b, s, 2 * aw), w['lam'], w['g_sub'], lambda_init).reshape(t, aw)
        ssm_tb = s
    y, h_re, h_im = _ssm(u.reshape(b, s, -1), h0_re.reshape(b, -1), h0_im.reshape(b, -1),
                         w['bw'], w['cw'], w['pw'], w['d_skip'], tb=ssm_tb)
    hres, xq, qp = _post(y.reshape(t, -1), o, x2, w['w_glu'], w['b_glu'], w['g_ssm'], w['w_out'], w['g_ffn'],
                         w['w_query'], tb=256)
    sel = _peer_select(qp, w['sub_keys'], tb=512)
    out = _peer(xq, sel, w['u_table'], w['v_table_t'], hres, tb=512, eb=1024)
    ngroups = w['nstate'] // SSM_STATE
    return (out.reshape(b, s, d), k.reshape(b, s, ATT_HEADS, 2, QK_DIM), v.reshape(b, s, ATT_HEADS, V_DIM),
            h_re.reshape(b, ngroups, SSM_STATE), h_im.reshape(b, ngroups, SSM_STATE))


def _prep_weights(l, g_mix, w_in, g_q, g_k, lam_q1, lam_k1, lam_q2, lam_k2, g_sub, a_re, a_im, log_dt, b_re,
                  b_im, c_re, c_im, d_skip, w_glu, b_glu, g_ssm, w_out, g_ffn, w_query, sub_keys, u_table,
                  v_table):
    aw = ATT_HEADS * V_DIM
    row = lambda a: a[l].reshape(1, -1)
    seg_id = jnp.arange(aw) // QK_DIM
    seg = jnp.where(seg_id[:, None] == seg_id[None, :], 1.0 / QK_DIM, 0.0).astype(BF16)
    bw, cw, pw = _ssm_params(a_re[l], a_im[l], log_dt[l], b_re[l], b_im[l], c_re[l], c_im[l])
    q_scale = QK_DIM ** -0.5 * LOG2_E
    score_bound = QK_DIM * q_scale * jnp.max(jnp.abs(g_q[l])) * jnp.max(jnp.abs(g_k[l])) * BF16_SLACK
    return dict(
        bounded=(score_bound <= MAX_UNSHIFTED_LOG2).astype(jnp.int32).reshape(1),
        g_mix=row(g_mix), w_in=w_in[l].astype(BF16), seg=seg,
        gq=jnp.tile(g_q[l], aw // QK_DIM).reshape(1, aw) * q_scale,
        gk=jnp.tile(g_k[l], aw // QK_DIM).reshape(1, aw),
        lam=(row(lam_q1), row(lam_k1), row(lam_q2), row(lam_k2)), g_sub=row(g_sub),
        bw=bw, cw=cw, pw=pw, d_skip=row(d_skip), nstate=a_re.shape[1] * a_re.shape[2],
        w_glu=w_glu[l].astype(BF16), b_glu=row(b_glu), g_ssm=row(g_ssm), w_out=w_out[l].astype(BF16),
        g_ffn=row(g_ffn), w_query=w_query[l].astype(BF16), sub_keys=sub_keys[l].astype(BF16),
        u_table=u_table[l].astype(BF16), v_table_t=v_table[l].T.astype(BF16))


def kernel(x_prompt, x_sample, cache_k, cache_v, state_ssm_re, state_ssm_im, g_mix, w_in, g_q, g_k, lam_q1, lam_k1, lam_q2, lam_k2, g_sub, a_re, a_im, log_dt, b_re, b_im, c_re, c_im, d_skip, w_glu, b_glu, g_ssm, w_out, g_ffn, w_query, sub_keys, u_table, v_table):
    depth = w_in.shape[0]
    yp, ys = x_prompt, x_sample
    outs = [[] for _ in range(8)]
    for l in range(depth):
        lambda_init = 0.8 - 0.6 * math.exp(-0.3 * l)
        w = _prep_weights(l, g_mix, w_in, g_q, g_k, lam_q1, lam_k1, lam_q2, lam_k2, g_sub, a_re, a_im, log_dt,
                          b_re, b_im, c_re, c_im, d_skip, w_glu, b_glu, g_ssm, w_out, g_ffn, w_query, sub_keys,
                          u_table, v_table)
        yp, kp, vp, hrp, hip = _layer(yp, None, None, None, None, lambda_init, w)
        ys, kk, vv, hrs, his = _layer(ys, cache_k[l], cache_v[l], state_ssm_re[l], state_ssm_im[l],
                                      lambda_init, w)
        for acc, val in zip(outs, (kp, vp, hrp, hip, kk, vv, hrs, his)):
            acc.append(val)
    return (yp, ys) + tuple(jnp.stack(o) for o in outs)
```

```python
import functools
import math

import jax
import jax.numpy as jnp
from jax import lax
from jax.experimental import pallas as pl
from jax.experimental.pallas import tpu as pltpu

F32 = jnp.float32
BF16 = jnp.bfloat16

CHUNK = 64
ATT_HEADS = 8
QK_DIM = 64
V_DIM = 2 * QK_DIM
SSM_GROUP = 16
SSM_STATE = 64
PEER_HEADS = 8
PEER_NKEYS = 128
PEER_TOPK = 16
NORM_EPS = 1e-6

LANES = 128
SUBLANES = 8
MXU_DIM = 256
GATE_ROWS = 32
SSM_SUPER = LANES // SSM_GROUP
SSM_SLAB = SSM_SUPER * SSM_STATE
VMEM_LIMIT = 56 * 1024 * 1024
SQRT_HALF = math.sqrt(0.5)
LOG2_E = math.log2(math.e)
BF16_SLACK = 1.0 + 2.0 ** -6
MAX_UNSHIFTED_LOG2 = 60.0

_NT = (((1,), (1,)), ((), ()))

BLOCKS = dict(
    proj_tokens=256,
    attn_queries=1024, attn_keys=2048, attn_rows=256,
    ssm_time=1024,
    select_tokens=512,
    peer_tokens=512, peer_experts=1024, peer_pieces=4,
)


def _params(*sem):
    return pltpu.CompilerParams(dimension_semantics=sem, vmem_limit_bytes=VMEM_LIMIT)


def _const_spec(shape):
    nd = len(shape)
    return pl.BlockSpec(shape, lambda *_: (0,) * nd, pipeline_mode=pl.Buffered(1))


def _gelu(x):
    return 0.5 * x * (1.0 + lax.erf(x * SQRT_HALF))


def _dot(a, b):
    return jnp.dot(a, b, preferred_element_type=F32)


def _in_proj_body(x_ref, gmix_ref, w_ref, seg_ref, gq_ref, gk_ref,
                  qb_ref, k_ref, kb_ref, v_ref, vb_ref, u_ref):
    aw = k_ref.shape[-1]
    x = x_ref[...]
    ms = jnp.mean(x * x, axis=-1, keepdims=True)
    xn = (x * lax.rsqrt(ms + NORM_EPS) * gmix_ref[...]).astype(BF16)

    def seg_rms(y, g):
        sq = y * y
        hi = sq.astype(BF16)
        lo = (sq - hi.astype(F32)).astype(BF16)
        mean = _dot(hi, seg_ref[...]) + _dot(lo, seg_ref[...])
        return y * lax.rsqrt(mean + NORM_EPS) * g

    q = seg_rms(_dot(xn, w_ref[:, 0:aw]), gq_ref[...])
    qb_ref[...] = q.astype(BF16)
    k = seg_rms(_dot(xn, w_ref[:, aw:2 * aw]), gk_ref[...])
    k_ref[...] = k
    kb_ref[...] = k.astype(BF16)
    v = _dot(xn, w_ref[:, 2 * aw:3 * aw])
    v_ref[...] = v
    vb = v.astype(BF16)
    ones = jnp.ones((vb.shape[0], V_DIM), BF16)
    for h in range(aw // V_DIM):
        vb_ref[:, 2 * h * V_DIM:(2 * h + 1) * V_DIM] = vb[:, h * V_DIM:(h + 1) * V_DIM]
        vb_ref[:, (2 * h + 1) * V_DIM:(2 * h + 2) * V_DIM] = ones
    u_ref[...] = _dot(xn, w_ref[:, 3 * aw:])


def _in_proj(x2, g_mix, w_in_b, seg, gq_t, gk_t, tb):
    t, d = x2.shape
    n = w_in_b.shape[1]
    aw = seg.shape[0]
    sw = n - 3 * aw
    tb = min(tb, t)
    row = lambda w: pl.BlockSpec((tb, w), lambda i: (i, 0))
    return pl.pallas_call(
        _in_proj_body,
        grid=(t // tb,),
        in_specs=[row(d), _const_spec((1, d)), _const_spec((d, n)), _const_spec((aw, aw)),
                  _const_spec((1, aw)), _const_spec((1, aw))],
        out_specs=[row(aw), row(aw), row(aw), row(aw), row(2 * aw), row(sw)],
        out_shape=[jax.ShapeDtypeStruct((t, aw), BF16), jax.ShapeDtypeStruct((t, aw), F32),
                   jax.ShapeDtypeStruct((t, aw), BF16), jax.ShapeDtypeStruct((t, aw), F32),
                   jax.ShapeDtypeStruct((t, 2 * aw), BF16), jax.ShapeDtypeStruct((t, sw), F32)],
        compiler_params=_params("parallel"),
        name="in_proj",
    )(x2, g_mix, w_in_b, seg, gq_t, gk_t)


def _lambda(lq1_ref, lk1_ref, lq2_ref, lk2_ref, lambda_init):
    s1 = jnp.sum(lq1_ref[...] * lk1_ref[...], axis=-1, keepdims=True)
    s2 = jnp.sum(lq2_ref[...] * lk2_ref[...], axis=-1, keepdims=True)
    return jnp.exp(s1) - jnp.exp(s2) + lambda_init


def _stack_maps(q):
    lane = lax.broadcasted_iota(jnp.int32, q.shape, 1)
    zero = jnp.zeros_like(q)
    return jnp.concatenate([jnp.where(lane < QK_DIM, q, zero), jnp.where(lane >= QK_DIM, q, zero)], axis=0)


def _sub_norm(o2, l, lam, gsub, nq, lambda_init):
    o2 = o2 / l
    o = o2[:nq] - lam * o2[nq:]
    ms = jnp.mean(o * o, axis=-1, keepdims=True)
    return (o * lax.rsqrt(ms + NORM_EPS) * gsub) * (1.0 - lambda_init)


def _attn_prompt_body(bounded_ref, lq1_ref, lk1_ref, lq2_ref, lk2_ref, gsub_ref, q_ref, k_ref, v_ref, o_ref,
                      q2_scr, m_scr, acc_scr, *, tq, tk, rt, lambda_init):
    i = pl.program_id(1)
    q2_scr[...] = _stack_maps(q_ref[...])
    acc_scr[...] = jnp.zeros(acc_scr.shape, F32)
    jd = (i * tq) // tk

    def scores(key0, nkeys, r0, masked):
        keys = pl.ds(pl.multiple_of(key0, tq), nkeys)
        s = lax.dot_general(q2_scr[r0:r0 + rt, :], k_ref[keys, :], _NT, preferred_element_type=F32)
        if masked:
            row = lax.broadcasted_iota(jnp.int32, s.shape, 0)
            col = lax.broadcasted_iota(jnp.int32, s.shape, 1)
            q_chunk = (i * tq + (r0 % tq) + row) // CHUNK
            k_chunk = (key0 + col) // CHUNK
            s = jnp.where(k_chunk <= q_chunk, s, -jnp.inf)
        return s, keys

    def unshifted_step(key0, nkeys, masked):
        for r0 in range(0, 2 * tq, rt):
            s, keys = scores(key0, nkeys, r0, masked)
            acc_scr[r0:r0 + rt, :] += _dot(jnp.exp2(s).astype(BF16), v_ref[keys, :])

    def online_step(key0, nkeys, masked):
        for r0 in range(0, 2 * tq, rt):
            rows = slice(r0, r0 + rt)
            s, keys = scores(key0, nkeys, r0, masked)
            m_prev = m_scr[rows, :]
            m_new = jnp.maximum(m_prev, jnp.max(s, axis=1, keepdims=True))
            p = jnp.exp2(s - m_new)
            acc_scr[rows, :] = jnp.exp2(m_prev - m_new) * acc_scr[rows, :] + _dot(p.astype(BF16), v_ref[keys, :])
            m_scr[rows, :] = m_new

    def sweep(step):
        def full_step(j, c):
            step(j * tk, tk, False)
            return c
        lax.fori_loop(0, jd, full_step, 0)
        if tk > tq:
            pl.when(i * tq > jd * tk)(lambda: step(jd * tk, tq, False))
        step(i * tq, tq, True)

    @pl.when(bounded_ref[0] == 1)
    def _():
        sweep(unshifted_step)

    @pl.when(bounded_ref[0] != 1)
    def _():
        m_scr[...] = jnp.full(m_scr.shape, -jnp.inf, F32)
        sweep(online_step)

    lam = _lambda(lq1_ref, lk1_ref, lq2_ref, lk2_ref, lambda_init)
    acc = acc_scr[...]
    o_ref[...] = _sub_norm(acc[:, :V_DIM], acc[:, V_DIM:V_DIM + 1], lam, gsub_ref[...], tq, lambda_init)


def _attn_prompt(bounded, qb, kb, vb1, lam_parts, g_sub, lambda_init, tq, tk, rt):
    s, aw = qb.shape
    tk = min(tk, s)
    tq = min(tq, tk)
    rt = min(rt, tq)
    assert s % tk == 0 and tk in (tq, 2 * tq) and tq % rt == 0 and rt % CHUNK == 0
    lam_specs = [_const_spec((1, QK_DIM))] * 4
    return pl.pallas_call(
        functools.partial(_attn_prompt_body, tq=tq, tk=tk, rt=rt, lambda_init=lambda_init),
        grid=(ATT_HEADS, s // tq),
        in_specs=[pl.BlockSpec(memory_space=pltpu.SMEM)] + lam_specs + [
            _const_spec((1, V_DIM)), pl.BlockSpec((tq, V_DIM), lambda h, i: (i, h)),
            pl.BlockSpec((s, V_DIM), lambda h, i: (0, h)), pl.BlockSpec((s, 2 * V_DIM), lambda h, i: (0, h))],
        out_specs=pl.BlockSpec((tq, V_DIM), lambda h, i: (i, h)),
        out_shape=jax.ShapeDtypeStruct((s, aw), F32),
        scratch_shapes=[pltpu.VMEM((2 * tq, V_DIM), BF16), pltpu.VMEM((2 * tq, 1), F32),
                        pltpu.VMEM((2 * tq, 2 * V_DIM), F32)],
        compiler_params=_params("parallel", "arbitrary"),
        name="attn_prompt",
    )(bounded, *lam_parts, g_sub, qb, kb, vb1)


def _attn_sample_body(lq1_ref, lk1_ref, lq2_ref, lk2_ref, gsub_ref, q_ref, kp_ref, vp_ref, kn_ref, vn_ref,
                      o_ref, *, past, lambda_init):
    nq = q_ref.shape[1]
    q2 = _stack_maps(q_ref[0])
    s_past = _dot(q2, kp_ref[0, 0].astype(BF16))
    v_past = vp_ref[0, pl.ds(pl.program_id(1), past, stride=ATT_HEADS), :]
    s_new = lax.dot_general(q2, kn_ref[0], _NT, preferred_element_type=F32)
    row = lax.broadcasted_iota(jnp.int32, s_new.shape, 0)
    col = lax.broadcasted_iota(jnp.int32, s_new.shape, 1)
    q_chunk = (past + jnp.where(row >= nq, row - nq, row)) // CHUNK
    s_new = jnp.where((past + col) // CHUNK <= q_chunk, s_new, -jnp.inf)
    m = jnp.maximum(jnp.max(s_past, axis=1, keepdims=True), jnp.max(s_new, axis=1, keepdims=True))
    p_past = jnp.exp2(s_past - m)
    p_new = jnp.exp2(s_new - m)
    l = jnp.sum(p_past, axis=1, keepdims=True) + jnp.sum(p_new, axis=1, keepdims=True)
    acc = _dot(p_past.astype(BF16), v_past.astype(BF16)) + _dot(p_new.astype(BF16), vn_ref[0])
    lam = _lambda(lq1_ref, lk1_ref, lq2_ref, lk2_ref, lambda_init)
    o_ref[0] = _sub_norm(acc, l, lam, gsub_ref[...], nq, lambda_init)


def _attn_sample(qb3, kpt, vp3, kb3, vb3, lam_parts, g_sub, lambda_init):
    b, s, aw = qb3.shape
    past = kpt.shape[-1]
    assert past % CHUNK == 0
    lam_specs = [_const_spec((1, QK_DIM))] * 4
    new = pl.BlockSpec((1, s, V_DIM), lambda bi, h: (bi, 0, h))
    new_v = pl.BlockSpec((1, s, V_DIM), lambda bi, h: (bi, 0, 2 * h))
    old_k = pl.BlockSpec((1, 1, 2 * QK_DIM, past), lambda bi, h: (bi, h, 0, 0))
    old_v = pl.BlockSpec((1, past * ATT_HEADS, V_DIM), lambda bi, h: (bi, 0, 0))
    return pl.pallas_call(
        functools.partial(_attn_sample_body, past=past, lambda_init=lambda_init),
        grid=(b, ATT_HEADS),
        in_specs=lam_specs + [_const_spec((1, V_DIM)), new, old_k, old_v, new, new_v],
        out_specs=new,
        out_shape=jax.ShapeDtypeStruct((b, s, aw), F32),
        compiler_params=_params("parallel", "arbitrary"),
        name="attn_sample",
    )(*lam_parts, g_sub, qb3, kpt, vp3, kb3, vb3)


def _ssm_param_body(are_ref, aim_ref, ldt_ref, bre_ref, bim_ref, abr_ref, abi_ref, bbr_ref, bbi_ref):
    ar, ai = are_ref[...], aim_ref[...]
    dt = jnp.exp(ldt_ref[...])
    mag = jnp.exp(ar * dt)
    abr, abi = mag * jnp.cos(ai * dt), mag * jnp.sin(ai * dt)
    den = ar * ar + ai * ai
    nr, ni = abr - 1.0, abi
    fr, fi = (nr * ar + ni * ai) / den, (ni * ar - nr * ai) / den
    br, bi = bre_ref[...], bim_ref[...]
    abr_ref[...] = abr
    abi_ref[...] = abi
    bbr_ref[...] = fr * br - fi * bi
    bbi_ref[...] = fr * bi + fi * br


def _ssm_params(a_re, a_im, log_dt, b_re, b_im, c_re, c_im):
    g, p = a_re.shape
    c = b_re.shape[-1]
    n = g * p
    col = lambda a: a.reshape(n, 1)
    abr, abi, bbr, bbi = pl.pallas_call(
        _ssm_param_body,
        out_shape=[jax.ShapeDtypeStruct((n, 1), F32)] * 2 + [jax.ShapeDtypeStruct((n, c), F32)] * 2,
        name="ssm_params",
    )(col(a_re), col(a_im), col(jnp.repeat(log_dt, p)), b_re.reshape(n, c), b_im.reshape(n, c))
    ns = g // SSM_SUPER
    eye = jnp.eye(SSM_SUPER, dtype=F32)

    def in_blocks(bb):
        bb = bb.reshape(ns, SSM_SUPER, p, c)
        return jnp.einsum('sgpc,gh->sgchp', bb, eye).reshape(ns, SSM_SUPER * c, SSM_SLAB)

    def out_blocks(cc):
        cc = cc.reshape(ns, SSM_SUPER, c, p)
        return jnp.einsum('sgcp,gh->sgphc', cc, eye).reshape(ns, SSM_SLAB, SSM_SUPER * c)

    bw = jnp.concatenate([in_blocks(bbr), in_blocks(bbi)], axis=2).astype(BF16)
    cw = jnp.concatenate([out_blocks(c_re), -out_blocks(c_im)], axis=1).astype(BF16)
    pr, pi = [abr.reshape(ns, 1, SSM_SLAB)], [abi.reshape(ns, 1, SSM_SLAB)]
    for _ in range(SUBLANES - 1):
        pr, pi = (pr + [pr[-1] * pr[0] - pi[-1] * pi[0]], pi + [pr[-1] * pi[0] + pi[-1] * pr[0]])
    rows = jnp.arange(SUBLANES).reshape(1, SUBLANES, 1)
    tabs = []
    for k in (1, 2, 4):
        tabs += [jnp.where(rows >= k, pr[k - 1], 0.0), jnp.where(rows >= k, pi[k - 1], 0.0)]
    tabs += [jnp.concatenate(pr, axis=1), jnp.concatenate(pi, axis=1)]
    return bw, cw, jnp.stack(tabs, axis=1)


def _ssm_body(u_ref, bw_ref, cw_ref, pw_ref, dsk_ref, h0r_ref, h0i_ref, y_ref, hr_ref, hi_ref,
              h_scr, c_scr):
    tb = pl.program_id(2)
    nrow = u_ref.shape[1]
    ns = SSM_SLAB

    @pl.when(tb == 0)
    def _():
        c_scr[0:1, :] = h0r_ref[0]
        c_scr[1:2, :] = h0i_ref[0]

    u = u_ref[0]
    h_scr[...] = _dot(u.astype(BF16), bw_ref[0])
    t1r, t1i, t2r, t2i, t4r, t4i, tcr, tci = (pw_ref[0, n] for n in range(8))

    def body(r, carry):
        cr, ci = carry
        rows = pl.ds(pl.multiple_of(r * SUBLANES, SUBLANES), SUBLANES)
        xr, xi = h_scr[rows, 0:ns], h_scr[rows, ns:2 * ns]
        for k, pr, pi in ((1, t1r, t1i), (2, t2r, t2i), (4, t4r, t4i)):
            sr, si = pltpu.roll(xr, k, 0), pltpu.roll(xi, k, 0)
            xr, xi = xr + pr * sr - pi * si, xi + pr * si + pi * sr
        xr, xi = xr + tcr * cr - tci * ci, xi + tcr * ci + tci * cr
        h_scr[rows, 0:ns] = xr
        h_scr[rows, ns:2 * ns] = xi
        return xr[SUBLANES - 1:SUBLANES], xi[SUBLANES - 1:SUBLANES]

    cr, ci = lax.fori_loop(0, nrow // SUBLANES, body, (c_scr[0:1, :], c_scr[1:2, :]), unroll=2)
    c_scr[0:1, :] = cr
    c_scr[1:2, :] = ci
    y_ref[0] = _dot(h_scr[...].astype(BF16), cw_ref[0]) + dsk_ref[...] * u

    @pl.when(tb == pl.num_programs(2) - 1)
    def _():
        hr_ref[0] = cr
        hi_ref[0] = ci


def _ssm(u3, h0r, h0i, bw, cw, pw, d_skip, tb):
    b, s, sw = u3.shape
    ns = sw // LANES
    tb = min(tb, s)
    assert s % tb == 0 and tb % SUBLANES == 0
    nstate = ns * SSM_SLAB
    slab = lambda r, c: pl.BlockSpec((1, r, c), lambda bi, g, t: (g, 0, 0))
    state = pl.BlockSpec((1, 1, SSM_SLAB), lambda bi, g, t: (bi, 0, g))
    seq = pl.BlockSpec((1, tb, LANES), lambda bi, g, t: (bi, t, g))
    return pl.pallas_call(
        _ssm_body,
        grid=(b, ns, s // tb),
        in_specs=[seq, slab(LANES, 2 * SSM_SLAB), slab(2 * SSM_SLAB, LANES),
                  pl.BlockSpec((1, 8, SUBLANES, SSM_SLAB), lambda bi, g, t: (g, 0, 0, 0)),
                  pl.BlockSpec((1, LANES), lambda bi, g, t: (0, g)), state, state],
        out_specs=[seq, state, state],
        out_shape=[jax.ShapeDtypeStruct((b, s, sw), F32), jax.ShapeDtypeStruct((b, 1, nstate), F32),
                   jax.ShapeDtypeStruct((b, 1, nstate), F32)],
        scratch_shapes=[pltpu.VMEM((tb, 2 * SSM_SLAB), F32), pltpu.VMEM((SUBLANES, SSM_SLAB), F32)],
        compiler_params=_params("parallel", "parallel", "arbitrary"),
        name="ssm",
    )(u3, bw, cw, pw, d_skip, h0r.reshape(b, 1, nstate), h0i.reshape(b, 1, nstate))


def _post_body(y_ref, o_ref, x_ref, wglu_ref, bglu_ref, gssm_ref, wout_ref, gffn_ref, wq_ref,
               h_ref, xq_ref, qp_ref):
    aw = o_ref.shape[-1]
    z = _gelu(y_ref[...])
    z = z * jax.nn.sigmoid(_dot(z.astype(BF16), wglu_ref[...]) + bglu_ref[...])
    ys = z * lax.rsqrt(jnp.mean(z * z, axis=-1, keepdims=True) + NORM_EPS) * gssm_ref[...]
    h = (x_ref[...] + _dot(o_ref[...].astype(BF16), wout_ref[0:aw, :])
         + _dot(ys.astype(BF16), wout_ref[aw:, :]))
    h_ref[...] = h
    xq = (h * lax.rsqrt(jnp.mean(h * h, axis=-1, keepdims=True) + NORM_EPS) * gffn_ref[...]).astype(BF16)
    xq_ref[...] = xq
    qp_ref[...] = _dot(xq, wq_ref[...]).astype(BF16)


def _post(y, o, x2, w_glu_b, b_glu, g_ssm, w_out_b, g_ffn, w_query_b, tb):
    t, d = x2.shape
    aw, sw = o.shape[1], y.shape[1]
    nq = w_query_b.shape[1]
    tb = min(tb, t)
    row = lambda w: pl.BlockSpec((tb, w), lambda i: (i, 0))
    return pl.pallas_call(
        _post_body,
        grid=(t // tb,),
        in_specs=[row(sw), row(aw), row(d), _const_spec((sw, sw)), _const_spec((1, sw)), _const_spec((1, sw)),
                  _const_spec((d, d)), _const_spec((1, d)), _const_spec((d, nq))],
        out_specs=[row(d), row(d), row(nq)],
        out_shape=[jax.ShapeDtypeStruct((t, d), F32), jax.ShapeDtypeStruct((t, d), BF16),
                   jax.ShapeDtypeStruct((t, nq), BF16)],
        compiler_params=_params("parallel"),
        name="post",
    )(y, o, x2, w_glu_b, b_glu, g_ssm, w_out_b, g_ffn, w_query_b)


def _top_rows(x, k):
    nrows = x.shape[0]
    rid = lax.broadcasted_iota(jnp.int32, x.shape, 0)
    vals, idx = [], []
    for _ in range(k):
        m = jnp.max(x, axis=0, keepdims=True)
        first = jnp.min(jnp.where(x == m, rid, nrows), axis=0, keepdims=True)
        x = jnp.where(rid == first, -jnp.inf, x)
        vals.append(m)
        idx.append(first)
    return vals, idx


def _peer_select_body(qp_ref, sk_ref, e0_ref, d_ref, s1_ref, e1_ref):
    nk = PEER_NKEYS
    s0 = lax.dot_general(sk_ref[0, 0], qp_ref[:, 0:nk], _NT, preferred_element_type=F32)
    s1 = lax.dot_general(sk_ref[0, 1], qp_ref[:, nk:2 * nk], _NT, preferred_element_type=F32)
    a, a_idx = _top_rows(s0, PEER_TOPK)
    b, _ = _top_rows(s1, PEER_TOPK)
    width = [PEER_TOPK // (r + 1) for r in range(PEER_TOPK)]
    cand = [[a[r] + b[c] for c in range(width[r])] for r in range(PEER_TOPK)]
    flat = [x for row in cand for x in row]
    pad = -len(flat) % SUBLANES
    f, _ = _top_rows(jnp.concatenate(flat + [jnp.full_like(a[0], -jnp.inf)] * pad, axis=0), PEER_TOPK)
    tau = f[PEER_TOPK - 1]
    z = jnp.exp(f[0] - f[0])
    for fv in f[1:]:
        z = z + jnp.exp(fv - f[0])
    rid = lax.broadcasted_iota(jnp.int32, s0.shape, 0)
    d = jnp.full(s0.shape, jnp.inf, F32)
    for r in range(PEER_TOPK):
        d_r = jnp.full_like(tau, jnp.inf)
        for c in range(width[r]):
            d_r = jnp.where(cand[r][c] >= tau, b[c], d_r)
        d = jnp.where(rid == a_idx[r], d_r, d)
    d_ref[0] = d
    s1_ref[0] = s1
    e0_ref[0] = jnp.exp(s0 - a[0])
    e1_ref[0] = jnp.exp(s1 - b[0]) / z


def _peer_select(qp, sub_keys_b, tb):
    t = qp.shape[0]
    nk = PEER_NKEYS
    tb = min(tb, t)
    tile = pl.BlockSpec((1, nk, tb), lambda i, h: (h, 0, i))
    full = jax.ShapeDtypeStruct((PEER_HEADS, nk, t), F32)
    return pl.pallas_call(
        _peer_select_body,
        grid=(t // tb, PEER_HEADS),
        in_specs=[pl.BlockSpec((tb, 2 * nk), lambda i, h: (i, h)),
                  pl.BlockSpec((1, 2, nk, nk), lambda i, h: (h, 0, 0, 0))],
        out_specs=[tile, tile, tile, tile],
        out_shape=[full, full, full, full],
        compiler_params=_params("parallel", "parallel"),
        name="peer_select",
    )(qp, sub_keys_b)


def _peer_body(xq_ref, e0_ref, d_ref, s1_ref, e1_ref, u_ref, vt_ref, h_ref, y_ref,
               acc_scr, wt_scr, *, nblocks, pieces):
    e = pl.program_id(1)
    nk = PEER_NKEYS
    eb, d = u_ref.shape
    tb = xq_ref.shape[0]
    per_block = eb // nk
    rows_a, rows_c = eb // pieces, d // pieces

    @pl.when(e == 0)
    def _():
        acc_scr[...] = jnp.zeros(acc_scr.shape, F32)
        wt_scr[...] = jnp.zeros(wt_scr.shape, BF16)

    blk = jnp.minimum(e, nblocks - 1)
    first_keys = pl.ds(pl.multiple_of(blk * per_block, per_block), per_block)

    def step(cur, prev):
        d_rows = [d_ref[h, first_keys, :] for h in range(PEER_HEADS)]
        e0_rows = [e0_ref[h, first_keys, :] for h in range(PEER_HEADS)]
        tw = min(MXU_DIM, tb)
        for t0 in range(0, tb, tw):
            toks = slice(t0, t0 + tw)
            for p in range(pieces):
                hid = lax.dot_general(u_ref[p * rows_a:(p + 1) * rows_a, :], xq_ref[toks, :], _NT,
                                      preferred_element_type=F32)
                slab = slice(p * rows_c, (p + 1) * rows_c)
                acc_scr[slab, toks] += _dot(vt_ref[slab, :], wt_scr[prev, :, toks])
                for il in range(rows_a // nk):
                    i = p * (rows_a // nk) + il
                    for c in range(0, tw, LANES):
                        lanes = slice(t0 + c, t0 + c + LANES)
                        for j0 in range(0, nk, GATE_ROWS):
                            keys = slice(j0, j0 + GATE_ROWS)
                            gate = jnp.zeros((GATE_ROWS, LANES), F32)
                            for h in range(PEER_HEADS):
                                sel = s1_ref[h, keys, lanes] >= d_rows[h][i:i + 1, lanes]
                                gate = gate + jnp.where(sel, e0_rows[h][i:i + 1, lanes] * e1_ref[h, keys, lanes], 0.0)
                            act = _gelu(hid[il * nk + j0:il * nk + j0 + GATE_ROWS, c:c + LANES])
                            rows = slice(p * rows_a + il * nk + j0, p * rows_a + il * nk + j0 + GATE_ROWS)
                            wt_scr[cur, rows, lanes] = (gate * act).astype(BF16)

    for parity in (0, 1):
        pl.when(e % 2 == parity)(functools.partial(step, parity, 1 - parity))

    @pl.when(e == pl.num_programs(1) - 1)
    def _():
        y_ref[...] = h_ref[...] + acc_scr[...].T


def _peer(xq, sel, u_b, vt_b, h, tb, eb):
    t, d = xq.shape
    ne = u_b.shape[0]
    nk = PEER_NKEYS
    tb = min(tb, t)
    nblocks = ne // eb
    single = pl.Buffered(1)
    tile = pl.BlockSpec((PEER_HEADS, nk, tb), lambda i, e: (0, 0, i), pipeline_mode=single)
    return pl.pallas_call(
        functools.partial(_peer_body, nblocks=nblocks, pieces=BLOCKS['peer_pieces']),
        grid=(t // tb, nblocks + 1),
        in_specs=[pl.BlockSpec((tb, d), lambda i, e: (i, 0), pipeline_mode=single),
                  tile, tile, tile, tile,
                  pl.BlockSpec((eb, d), lambda i, e: (jnp.minimum(e, nblocks - 1), 0)),
                  pl.BlockSpec((d, eb), lambda i, e: (0, jnp.maximum(e - 1, 0))),
                  pl.BlockSpec((tb, d), lambda i, e: (i, 0), pipeline_mode=single)],
        out_specs=pl.BlockSpec((tb, d), lambda i, e: (i, 0)),
        out_shape=jax.ShapeDtypeStruct((t, d), F32),
        scratch_shapes=[pltpu.VMEM((d, tb), F32), pltpu.VMEM((2, eb, tb), BF16)],
        compiler_params=_params("parallel", "arbitrary"),
        name="peer",
    )(xq, *sel, u_b, vt_b, h)


def _layer(x, k_past, v_past, h0_re, h0_im, lambda_init, w):
    b, s, d = x.shape
    t = b * s
    aw = ATT_HEADS * V_DIM
    x2 = x.reshape(t, d)
    qb, k, kb, v, vb, u = _in_proj(x2, w['g_mix'], w['w_in'], w['seg'], w['gq'], w['gk'], tb=BLOCKS['proj_tokens'])
    if k_past is None:
        assert b == 1
        o = _attn_prompt(w['bounded'], qb, kb, vb, w['lam'], w['g_sub'], lambda_init,
                         tq=BLOCKS['attn_queries'], tk=BLOCKS['attn_keys'], rt=BLOCKS['attn_rows'])
        h0_re = jnp.zeros((b, w['nstate']), F32)
        h0_im = h0_re
        ssm_tb = BLOCKS['ssm_time']
    else:
        past = k_past.shape[1]
        r3 = lambda a: a.reshape(b, s, aw)
        k_t = jnp.transpose(k_past, (0, 2, 3, 4, 1)).reshape(b, ATT_HEADS, 2 * QK_DIM, past)
        o = _attn_sample(r3(qb), k_t, v_past.reshape(b, past * ATT_HEADS, V_DIM), r3(kb),
                         vb.reshape(b, s, 2 * aw), w['lam'], w['g_sub'], lambda_init).reshape(t, aw)
        ssm_tb = s
    y, h_re, h_im = _ssm(u.reshape(b, s, -1), h0_re.reshape(b, -1), h0_im.reshape(b, -1),
                         w['bw'], w['cw'], w['pw'], w['d_skip'], tb=ssm_tb)
    hres, xq, qp = _post(y.reshape(t, -1), o, x2, w['w_glu'], w['b_glu'], w['g_ssm'], w['w_out'], w['g_ffn'],
                         w['w_query'], tb=BLOCKS['proj_tokens'])
    sel = _peer_select(qp, w['sub_keys'], tb=BLOCKS['select_tokens'])
    out = _peer(xq, sel, w['u_table'], w['v_table_t'], hres, tb=BLOCKS['peer_tokens'], eb=BLOCKS['peer_experts'])
    ngroups = w['nstate'] // SSM_STATE
    return (out.reshape(b, s, d), k.reshape(b, s, ATT_HEADS, 2, QK_DIM), v.reshape(b, s, ATT_HEADS, V_DIM),
            h_re.reshape(b, ngroups, SSM_STATE), h_im.reshape(b, ngroups, SSM_STATE))


def _prep_weights(l, g_mix, w_in, g_q, g_k, lam_q1, lam_k1, lam_q2, lam_k2, g_sub, a_re, a_im, log_dt, b_re,
                  b_im, c_re, c_im, d_skip, w_glu, b_glu, g_ssm, w_out, g_ffn, w_query, sub_keys, u_table,
                  v_table):
    aw = ATT_HEADS * V_DIM
    row = lambda a: a[l].reshape(1, -1)
    seg_id = jnp.arange(aw) // QK_DIM
    seg = jnp.where(seg_id[:, None] == seg_id[None, :], 1.0 / QK_DIM, 0.0).astype(BF16)
    bw, cw, pw = _ssm_params(a_re[l], a_im[l], log_dt[l], b_re[l], b_im[l], c_re[l], c_im[l])
    q_scale = QK_DIM ** -0.5 * LOG2_E
    score_bound = QK_DIM * q_scale * jnp.max(jnp.abs(g_q[l])) * jnp.max(jnp.abs(g_k[l])) * BF16_SLACK
    return dict(
        bounded=(score_bound <= MAX_UNSHIFTED_LOG2).astype(jnp.int32).reshape(1),
        g_mix=row(g_mix), w_in=w_in[l].astype(BF16), seg=seg,
        gq=jnp.tile(g_q[l], aw // QK_DIM).reshape(1, aw) * q_scale,
        gk=jnp.tile(g_k[l], aw // QK_DIM).reshape(1, aw),
        lam=(row(lam_q1), row(lam_k1), row(lam_q2), row(lam_k2)), g_sub=row(g_sub),
        bw=bw, cw=cw, pw=pw, d_skip=row(d_skip), nstate=a_re.shape[1] * a_re.shape[2],
        w_glu=w_glu[l].astype(BF16), b_glu=row(b_glu), g_ssm=row(g_ssm), w_out=w_out[l].astype(BF16),
        g_ffn=row(g_ffn), w_query=w_query[l].astype(BF16), sub_keys=sub_keys[l].astype(BF16),
        u_table=u_table[l].astype(BF16), v_table_t=v_table[l].T.astype(BF16))


def kernel(x_prompt, x_sample, cache_k, cache_v, state_ssm_re, state_ssm_im, g_mix, w_in, g_q, g_k, lam_q1, lam_k1, lam_q2, lam_k2, g_sub, a_re, a_im, log_dt, b_re, b_im, c_re, c_im, d_skip, w_glu, b_glu, g_ssm, w_out, g_ffn, w_query, sub_keys, u_table, v_table):
    depth = w_in.shape[0]
    yp, ys = x_prompt, x_sample
    outs = [[] for _ in range(8)]
    for l in range(depth):
        lambda_init = 0.8 - 0.6 * math.exp(-0.3 * l)
        w = _prep_weights(l, g_mix, w_in, g_q, g_k, lam_q1, lam_k1, lam_q2, lam_k2, g_sub, a_re, a_im, log_dt,
                          b_re, b_im, c_re, c_im, d_skip, w_glu, b_glu, g_ssm, w_out, g_ffn, w_query, sub_keys,
                          u_table, v_table)
        yp, kp, vp, hrp, hip = _layer(yp, None, None, None, None, lambda_init, w)
        ys, kk, vv, hrs, his = _layer(ys, cache_k[l], cache_v[l], state_ssm_re[l], state_ssm_im[l],
                                      lambda_init, w)
        for acc, val in zip(outs, (kp, vp, hrp, hip, kk, vv, hrs, his)):
            acc.append(val)
    return (yp, ys) + tuple(jnp.stack(o) for o in outs)
```

```python
import functools
import math

import jax
import jax.numpy as jnp
from jax import lax
from jax.experimental import pallas as pl
from jax.experimental.pallas import tpu as pltpu

F32 = jnp.float32
BF16 = jnp.bfloat16

CHUNK = 64
ATT_HEADS = 8
QK_DIM = 64
V_DIM = 2 * QK_DIM
SSM_GROUP = 16
SSM_STATE = 64
PEER_HEADS = 8
PEER_NKEYS = 128
PEER_TOPK = 16
NORM_EPS = 1e-6

LANES = 128
SUBLANES = 8
MXU_DIM = 256
GATE_ROWS = 32
SSM_SUPER = LANES // SSM_GROUP
SSM_SLAB = SSM_SUPER * SSM_STATE
VMEM_LIMIT = 56 * 1024 * 1024
SQRT_HALF = math.sqrt(0.5)
LOG2_E = math.log2(math.e)
BF16_SLACK = 1.0 + 2.0 ** -6
MAX_UNSHIFTED_LOG2 = 60.0

_NT = (((1,), (1,)), ((), ()))

BLOCKS = dict(
    proj_tokens=256,
    attn_queries=1024, attn_keys=2048, attn_rows=256,
    ssm_time=1024,
    select_tokens=512,
    peer_tokens=512, peer_experts=1024, peer_pieces=4,
)


def _params(*sem):
    return pltpu.CompilerParams(dimension_semantics=sem, vmem_limit_bytes=VMEM_LIMIT)


def _const_spec(shape):
    nd = len(shape)
    return pl.BlockSpec(shape, lambda *_: (0,) * nd, pipeline_mode=pl.Buffered(1))


def _gelu(x):
    return 0.5 * x * (1.0 + lax.erf(x * SQRT_HALF))


def _dot(a, b):
    return jnp.dot(a, b, preferred_element_type=F32)


def _in_proj_body(x_ref, gmix_ref, w_ref, seg_ref, gq_ref, gk_ref,
                  qb_ref, k_ref, kb_ref, v_ref, vb_ref, u_ref):
    aw = k_ref.shape[-1]
    x = x_ref[...]
    ms = jnp.mean(x * x, axis=-1, keepdims=True)
    xn = (x * lax.rsqrt(ms + NORM_EPS) * gmix_ref[...]).astype(BF16)

    def seg_rms(y, g):
        sq = y * y
        hi = sq.astype(BF16)
        lo = (sq - hi.astype(F32)).astype(BF16)
        mean = _dot(hi, seg_ref[...]) + _dot(lo, seg_ref[...])
        return y * lax.rsqrt(mean + NORM_EPS) * g

    q = seg_rms(_dot(xn, w_ref[:, 0:aw]), gq_ref[...])
    qb_ref[...] = q.astype(BF16)
    k = seg_rms(_dot(xn, w_ref[:, aw:2 * aw]), gk_ref[...])
    k_ref[...] = k
    kb_ref[...] = k.astype(BF16)
    v = _dot(xn, w_ref[:, 2 * aw:3 * aw])
    v_ref[...] = v
    vb = v.astype(BF16)
    ones = jnp.ones((vb.shape[0], V_DIM), BF16)
    for h in range(aw // V_DIM):
        vb_ref[:, 2 * h * V_DIM:(2 * h + 1) * V_DIM] = vb[:, h * V_DIM:(h + 1) * V_DIM]
        vb_ref[:, (2 * h + 1) * V_DIM:(2 * h + 2) * V_DIM] = ones
    u_ref[...] = _dot(xn, w_ref[:, 3 * aw:])


def _in_proj(x2, g_mix, w_in_b, seg, gq_t, gk_t, tb):
    t, d = x2.shape
    n = w_in_b.shape[1]
    aw = seg.shape[0]
    sw = n - 3 * aw
    tb = min(tb, t)
    row = lambda w: pl.BlockSpec((tb, w), lambda i: (i, 0))
    return pl.pallas_call(
        _in_proj_body,
        grid=(t // tb,),
        in_specs=[row(d), _const_spec((1, d)), _const_spec((d, n)), _const_spec((aw, aw)),
                  _const_spec((1, aw)), _const_spec((1, aw))],
        out_specs=[row(aw), row(aw), row(aw), row(aw), row(2 * aw), row(sw)],
        out_shape=[jax.ShapeDtypeStruct((t, aw), BF16), jax.ShapeDtypeStruct((t, aw), F32),
                   jax.ShapeDtypeStruct((t, aw), BF16), jax.ShapeDtypeStruct((t, aw), F32),
                   jax.ShapeDtypeStruct((t, 2 * aw), BF16), jax.ShapeDtypeStruct((t, sw), F32)],
        compiler_params=_params("parallel"),
        name="in_proj",
    )(x2, g_mix, w_in_b, seg, gq_t, gk_t)


def _lambda(lq1_ref, lk1_ref, lq2_ref, lk2_ref, lambda_init):
    s1 = jnp.sum(lq1_ref[...] * lk1_ref[...], axis=-1, keepdims=True)
    s2 = jnp.sum(lq2_ref[...] * lk2_ref[...], axis=-1, keepdims=True)
    return jnp.exp(s1) - jnp.exp(s2) + lambda_init


def _stack_maps(q):
    lane = lax.broadcasted_iota(jnp.int32, q.shape, 1)
    zero = jnp.zeros_like(q)
    return jnp.concatenate([jnp.where(lane < QK_DIM, q, zero), jnp.where(lane >= QK_DIM, q, zero)], axis=0)


def _sub_norm(o2, l, lam, gsub, nq, lambda_init):
    o2 = o2 / l
    o = o2[:nq] - lam * o2[nq:]
    ms = jnp.mean(o * o, axis=-1, keepdims=True)
    return (o * lax.rsqrt(ms + NORM_EPS) * gsub) * (1.0 - lambda_init)


def _attn_prompt_body(bounded_ref, lq1_ref, lk1_ref, lq2_ref, lk2_ref, gsub_ref, q_ref, k_ref, v_ref, o_ref,
                      q2_scr, m_scr, acc_scr, *, tq, tk, rt, lambda_init):
    i = pl.program_id(1)
    q2_scr[...] = _stack_maps(q_ref[...])
    acc_scr[...] = jnp.zeros(acc_scr.shape, F32)
    jd = (i * tq) // tk

    def scores(key0, nkeys, r0, masked):
        keys = pl.ds(pl.multiple_of(key0, tq), nkeys)
        s = lax.dot_general(q2_scr[r0:r0 + rt, :], k_ref[keys, :], _NT, preferred_element_type=F32)
        if masked:
            row = lax.broadcasted_iota(jnp.int32, s.shape, 0)
            col = lax.broadcasted_iota(jnp.int32, s.shape, 1)
            q_chunk = (i * tq + (r0 % tq) + row) // CHUNK
            k_chunk = (key0 + col) // CHUNK
            s = jnp.where(k_chunk <= q_chunk, s, -jnp.inf)
        return s, keys

    def unshifted_step(key0, nkeys, masked):
        for r0 in range(0, 2 * tq, rt):
            s, keys = scores(key0, nkeys, r0, masked)
            acc_scr[r0:r0 + rt, :] += _dot(jnp.exp2(s).astype(BF16), v_ref[keys, :])

    def online_step(key0, nkeys, masked):
        for r0 in range(0, 2 * tq, rt):
            rows = slice(r0, r0 + rt)
            s, keys = scores(key0, nkeys, r0, masked)
            m_prev = m_scr[rows, :]
            m_new = jnp.maximum(m_prev, jnp.max(s, axis=1, keepdims=True))
            p = jnp.exp2(s - m_new)
            acc_scr[rows, :] = jnp.exp2(m_prev - m_new) * acc_scr[rows, :] + _dot(p.astype(BF16), v_ref[keys, :])
            m_scr[rows, :] = m_new

    def sweep(step):
        def full_step(j, c):
            step(j * tk, tk, False)
            return c
        lax.fori_loop(0, jd, full_step, 0)
        if tk > tq:
            pl.when(i * tq > jd * tk)(lambda: step(jd * tk, tq, False))
        step(i * tq, tq, True)

    @pl.when(bounded_ref[0] == 1)
    def _():
        sweep(unshifted_step)

    @pl.when(bounded_ref[0] != 1)
    def _():
        m_scr[...] = jnp.full(m_scr.shape, -jnp.inf, F32)
        sweep(online_step)

    lam = _lambda(lq1_ref, lk1_ref, lq2_ref, lk2_ref, lambda_init)
    acc = acc_scr[...]
    o_ref[...] = _sub_norm(acc[:, :V_DIM], acc[:, V_DIM:V_DIM + 1], lam, gsub_ref[...], tq, lambda_init)


def _attn_prompt(bounded, qb, kb, vb1, lam_parts, g_sub, lambda_init, tq, tk, rt):
    s, aw = qb.shape
    tk = min(tk, s)
    tq = min(tq, tk)
    rt = min(rt, tq)
    assert s % tk == 0 and tk in (tq, 2 * tq) and tq % rt == 0 and rt % CHUNK == 0
    lam_specs = [_const_spec((1, QK_DIM))] * 4
    return pl.pallas_call(
        functools.partial(_attn_prompt_body, tq=tq, tk=tk, rt=rt, lambda_init=lambda_init),
        grid=(ATT_HEADS, s // tq),
        in_specs=[pl.BlockSpec(memory_space=pltpu.SMEM)] + lam_specs + [
            _const_spec((1, V_DIM)), pl.BlockSpec((tq, V_DIM), lambda h, i: (i, h)),
            pl.BlockSpec((s, V_DIM), lambda h, i: (0, h)), pl.BlockSpec((s, 2 * V_DIM), lambda h, i: (0, h))],
        out_specs=pl.BlockSpec((tq, V_DIM), lambda h, i: (i, h)),
        out_shape=jax.ShapeDtypeStruct((s, aw), F32),
        scratch_shapes=[pltpu.VMEM((2 * tq, V_DIM), BF16), pltpu.VMEM((2 * tq, 1), F32),
                        pltpu.VMEM((2 * tq, 2 * V_DIM), F32)],
        compiler_params=_params("parallel", "arbitrary"),
        name="attn_prompt",
    )(bounded, *lam_parts, g_sub, qb, kb, vb1)


def _attn_sample_body(lq1_ref, lk1_ref, lq2_ref, lk2_ref, gsub_ref, q_ref, kp_ref, vp_ref, kn_ref, vn_ref,
                      o_ref, *, past, lambda_init):
    nq = q_ref.shape[1]
    q2 = _stack_maps(q_ref[0])
    s_past = _dot(q2, kp_ref[0, 0].astype(BF16))
    v_past = vp_ref[0, pl.ds(pl.program_id(1), past, stride=ATT_HEADS), :]
    s_new = lax.dot_general(q2, kn_ref[0], _NT, preferred_element_type=F32)
    row = lax.broadcasted_iota(jnp.int32, s_new.shape, 0)
    col = lax.broadcasted_iota(jnp.int32, s_new.shape, 1)
    q_chunk = (past + jnp.where(row >= nq, row - nq, row)) // CHUNK
    s_new = jnp.where((past + col) // CHUNK <= q_chunk, s_new, -jnp.inf)
    m = jnp.maximum(jnp.max(s_past, axis=1, keepdims=True), jnp.max(s_new, axis=1, keepdims=True))
    p_past = jnp.exp2(s_past - m)
    p_new = jnp.exp2(s_new - m)
    l = jnp.sum(p_past, axis=1, keepdims=True) + jnp.sum(p_new, axis=1, keepdims=True)
    acc = _dot(p_past.astype(BF16), v_past.astype(BF16)) + _dot(p_new.astype(BF16), vn_ref[0])
    lam = _lambda(lq1_ref, lk1_ref, lq2_ref, lk2_ref, lambda_init)
    o_ref[0] = _sub_norm(acc, l, lam, gsub_ref[...], nq, lambda_init)


def _attn_sample(qb3, kpt, vp3, kb3, vb3, lam_parts, g_sub, lambda_init):
    b, s, aw = qb3.shape
    past = kpt.shape[-1]
    assert past % CHUNK == 0
    lam_specs = [_const_spec((1, QK_DIM))] * 4
    new = pl.BlockSpec((1, s, V_DIM), lambda bi, h: (bi, 0, h))
    new_v = pl.BlockSpec((1, s, V_DIM), lambda bi, h: (bi, 0, 2 * h))
    old_k = pl.BlockSpec((1, 1, 2 * QK_DIM, past), lambda bi, h: (bi, h, 0, 0))
    old_v = pl.BlockSpec((1, past * ATT_HEADS, V_DIM), lambda bi, h: (bi, 0, 0))
    return pl.pallas_call(
        functools.partial(_attn_sample_body, past=past, lambda_init=lambda_init),
        grid=(b, ATT_HEADS),
        in_specs=lam_specs + [_const_spec((1, V_DIM)), new, old_k, old_v, new, new_v],
        out_specs=new,
        out_shape=jax.ShapeDtypeStruct((b, s, aw), F32),
        compiler_params=_params("parallel", "arbitrary"),
        name="attn_sample",
    )(*lam_parts, g_sub, qb3, kpt, vp3, kb3, vb3)


def _ssm_param_body(are_ref, aim_ref, ldt_ref, bre_ref, bim_ref, abr_ref, abi_ref, bbr_ref, bbi_ref):
    ar, ai = are_ref[...], aim_ref[...]
    dt = jnp.exp(ldt_ref[...])
    mag = jnp.exp(ar * dt)
    abr, abi = mag * jnp.cos(ai * dt), mag * jnp.sin(ai * dt)
    den = ar * ar + ai * ai
    nr, ni = abr - 1.0, abi
    fr, fi = (nr * ar + ni * ai) / den, (ni * ar - nr * ai) / den
    br, bi = bre_ref[...], bim_ref[...]
    abr_ref[...] = abr
    abi_ref[...] = abi
    bbr_ref[...] = fr * br - fi * bi
    bbi_ref[...] = fr * bi + fi * br


def _ssm_params(a_re, a_im, log_dt, b_re, b_im, c_re, c_im):
    g, p = a_re.shape
    c = b_re.shape[-1]
    n = g * p
    col = lambda a: a.reshape(n, 1)
    abr, abi, bbr, bbi = pl.pallas_call(
        _ssm_param_body,
        out_shape=[jax.ShapeDtypeStruct((n, 1), F32)] * 2 + [jax.ShapeDtypeStruct((n, c), F32)] * 2,
        name="ssm_params",
    )(col(a_re), col(a_im), col(jnp.repeat(log_dt, p)), b_re.reshape(n, c), b_im.reshape(n, c))
    ns = g // SSM_SUPER
    eye = jnp.eye(SSM_SUPER, dtype=F32)

    def in_blocks(bb):
        bb = bb.reshape(ns, SSM_SUPER, p, c)
        return jnp.einsum('sgpc,gh->sgchp', bb, eye).reshape(ns, SSM_SUPER * c, SSM_SLAB)

    def out_blocks(cc):
        cc = cc.reshape(ns, SSM_SUPER, c, p)
        return jnp.einsum('sgcp,gh->sgphc', cc, eye).reshape(ns, SSM_SLAB, SSM_SUPER * c)

    bw = jnp.concatenate([in_blocks(bbr), in_blocks(bbi)], axis=2).astype(BF16)
    cw = jnp.concatenate([out_blocks(c_re), -out_blocks(c_im)], axis=1).astype(BF16)
    pr, pi = [abr.reshape(ns, 1, SSM_SLAB)], [abi.reshape(ns, 1, SSM_SLAB)]
    for _ in range(SUBLANES - 1):
        pr, pi = (pr + [pr[-1] * pr[0] - pi[-1] * pi[0]], pi + [pr[-1] * pi[0] + pi[-1] * pr[0]])
    rows = jnp.arange(SUBLANES).reshape(1, SUBLANES, 1)
    tabs = []
    for k in (1, 2, 4):
        tabs += [jnp.where(rows >= k, pr[k - 1], 0.0), jnp.where(rows >= k, pi[k - 1], 0.0)]
    tabs += [jnp.concatenate(pr, axis=1), jnp.concatenate(pi, axis=1)]
    return bw, cw, jnp.stack(tabs, axis=1)


def _ssm_body(u_ref, bw_ref, cw_ref, pw_ref, dsk_ref, h0r_ref, h0i_ref, y_ref, hr_ref, hi_ref,
              h_scr, c_scr):
    tb = pl.program_id(2)
    nrow = u_ref.shape[1]
    ns = SSM_SLAB

    @pl.when(tb == 0)
    def _():
        c_scr[0:1, :] = h0r_ref[0]
        c_scr[1:2, :] = h0i_ref[0]

    u = u_ref[0]
    h_scr[...] = _dot(u.astype(BF16), bw_ref[0])
    t1r, t1i, t2r, t2i, t4r, t4i, tcr, tci = (pw_ref[0, n] for n in range(8))

    def body(r, carry):
        cr, ci = carry
        rows = pl.ds(pl.multiple_of(r * SUBLANES, SUBLANES), SUBLANES)
        xr, xi = h_scr[rows, 0:ns], h_scr[rows, ns:2 * ns]
        for k, pr, pi in ((1, t1r, t1i), (2, t2r, t2i), (4, t4r, t4i)):
            sr, si = pltpu.roll(xr, k, 0), pltpu.roll(xi, k, 0)
            xr, xi = xr + pr * sr - pi * si, xi + pr * si + pi * sr
        xr, xi = xr + tcr * cr - tci * ci, xi + tcr * ci + tci * cr
        h_scr[rows, 0:ns] = xr
        h_scr[rows, ns:2 * ns] = xi
        return xr[SUBLANES - 1:SUBLANES], xi[SUBLANES - 1:SUBLANES]

    cr, ci = lax.fori_loop(0, nrow // SUBLANES, body, (c_scr[0:1, :], c_scr[1:2, :]), unroll=2)
    c_scr[0:1, :] = cr
    c_scr[1:2, :] = ci
    y_ref[0] = _dot(h_scr[...].astype(BF16), cw_ref[0]) + dsk_ref[...] * u

    @pl.when(tb == pl.num_programs(2) - 1)
    def _():
        hr_ref[0] = cr
        hi_ref[0] = ci


def _ssm(u3, h0r, h0i, bw, cw, pw, d_skip, tb):
    b, s, sw = u3.shape
    ns = sw // LANES
    tb = min(tb, s)
    assert s % tb == 0 and tb % SUBLANES == 0
    nstate = ns * SSM_SLAB
    slab = lambda r, c: pl.BlockSpec((1, r, c), lambda bi, g, t: (g, 0, 0))
    state = pl.BlockSpec((1, 1, SSM_SLAB), lambda bi, g, t: (bi, 0, g))
    seq = pl.BlockSpec((1, tb, LANES), lambda bi, g, t: (bi, t, g))
    return pl.pallas_call(
        _ssm_body,
        grid=(b, ns, s // tb),
        in_specs=[seq, slab(LANES, 2 * SSM_SLAB), slab(2 * SSM_SLAB, LANES),
                  pl.BlockSpec((1, 8, SUBLANES, SSM_SLAB), lambda bi, g, t: (g, 0, 0, 0)),
                  pl.BlockSpec((1, LANES), lambda bi, g, t: (0, g)), state, state],
        out_specs=[seq, state, state],
        out_shape=[jax.ShapeDtypeStruct((b, s, sw), F32), jax.ShapeDtypeStruct((b, 1, nstate), F32),
                   jax.ShapeDtypeStruct((b, 1, nstate), F32)],
        scratch_shapes=[pltpu.VMEM((tb, 2 * SSM_SLAB), F32), pltpu.VMEM((SUBLANES, SSM_SLAB), F32)],
        compiler_params=_params("parallel", "parallel", "arbitrary"),
        name="ssm",
    )(u3, bw, cw, pw, d_skip, h0r.reshape(b, 1, nstate), h0i.reshape(b, 1, nstate))


def _post_body(y_ref, o_ref, x_ref, wglu_ref, bglu_ref, gssm_ref, wout_ref, gffn_ref, wq_ref,
               h_ref, xq_ref, qp_ref):
    aw = o_ref.shape[-1]
    z = _gelu(y_ref[...])
    z = z * jax.nn.sigmoid(_dot(z.astype(BF16), wglu_ref[...]) + bglu_ref[...])
    ys = z * lax.rsqrt(jnp.mean(z * z, axis=-1, keepdims=True) + NORM_EPS) * gssm_ref[...]
    h = (x_ref[...] + _dot(o_ref[...].astype(BF16), wout_ref[0:aw, :])
         + _dot(ys.astype(BF16), wout_ref[aw:, :]))
    h_ref[...] = h
    xq = (h * lax.rsqrt(jnp.mean(h * h, axis=-1, keepdims=True) + NORM_EPS) * gffn_ref[...]).astype(BF16)
    xq_ref[...] = xq
    qp_ref[...] = _dot(xq, wq_ref[...]).astype(BF16)


def _post(y, o, x2, w_glu_b, b_glu, g_ssm, w_out_b, g_ffn, w_query_b, tb):
    t, d = x2.shape
    aw, sw = o.shape[1], y.shape[1]
    nq = w_query_b.shape[1]
    tb = min(tb, t)
    row = lambda w: pl.BlockSpec((tb, w), lambda i: (i, 0))
    return pl.pallas_call(
        _post_body,
        grid=(t // tb,),
        in_specs=[row(sw), row(aw), row(d), _const_spec((sw, sw)), _const_spec((1, sw)), _const_spec((1, sw)),
                  _const_spec((d, d)), _const_spec((1, d)), _const_spec((d, nq))],
        out_specs=[row(d), row(d), row(nq)],
        out_shape=[jax.ShapeDtypeStruct((t, d), F32), jax.ShapeDtypeStruct((t, d), BF16),
                   jax.ShapeDtypeStruct((t, nq), BF16)],
        compiler_params=_params("parallel"),
        name="post",
    )(y, o, x2, w_glu_b, b_glu, g_ssm, w_out_b, g_ffn, w_query_b)


def _top_rows(x, k):
    nrows = x.shape[0]
    rid = lax.broadcasted_iota(jnp.int32, x.shape, 0)
    vals, idx = [], []
    for _ in range(k):
        m = jnp.max(x, axis=0, keepdims=True)
        first = jnp.min(jnp.where(x == m, rid, nrows), axis=0, keepdims=True)
        x = jnp.where(rid == first, -jnp.inf, x)
        vals.append(m)
        idx.append(first)
    return vals, idx


def _peer_select_body(qp_ref, sk_ref, e0_ref, d_ref, s1_ref, e1_ref):
    nk = PEER_NKEYS
    s0 = lax.dot_general(sk_ref[0, 0], qp_ref[:, 0:nk], _NT, preferred_element_type=F32)
    s1 = lax.dot_general(sk_ref[0, 1], qp_ref[:, nk:2 * nk], _NT, preferred_element_type=F32)
    a, a_idx = _top_rows(s0, PEER_TOPK)
    b, _ = _top_rows(s1, PEER_TOPK)
    width = [PEER_TOPK // (r + 1) for r in range(PEER_TOPK)]
    cand = [[a[r] + b[c] for c in range(width[r])] for r in range(PEER_TOPK)]
    flat = [x for row in cand for x in row]
    pad = -len(flat) % SUBLANES
    f, _ = _top_rows(jnp.concatenate(flat + [jnp.full_like(a[0], -jnp.inf)] * pad, axis=0), PEER_TOPK)
    tau = f[PEER_TOPK - 1]
    z = jnp.exp(f[0] - f[0])
    for fv in f[1:]:
        z = z + jnp.exp(fv - f[0])
    rid = lax.broadcasted_iota(jnp.int32, s0.shape, 0)
    d = jnp.full(s0.shape, jnp.inf, F32)
    for r in range(PEER_TOPK):
        d_r = jnp.full_like(tau, jnp.inf)
        for c in range(width[r]):
            d_r = jnp.where(cand[r][c] >= tau, b[c], d_r)
        d = jnp.where(rid == a_idx[r], d_r, d)
    d_ref[0] = d
    s1_ref[0] = s1
    e0_ref[0] = jnp.exp(s0 - a[0])
    e1_ref[0] = jnp.exp(s1 - b[0]) / z


def _peer_select(qp, sub_keys_b, tb):
    t = qp.shape[0]
    nk = PEER_NKEYS
    tb = min(tb, t)
    tile = pl.BlockSpec((1, nk, tb), lambda i, h: (h, 0, i))
    full = jax.ShapeDtypeStruct((PEER_HEADS, nk, t), F32)
    return pl.pallas_call(
        _peer_select_body,
        grid=(t // tb, PEER_HEADS),
        in_specs=[pl.BlockSpec((tb, 2 * nk), lambda i, h: (i, h)),
                  pl.BlockSpec((1, 2, nk, nk), lambda i, h: (h, 0, 0, 0))],
        out_specs=[tile, tile, tile, tile],
        out_shape=[full, full, full, full],
        compiler_params=_params("parallel", "parallel"),
        name="peer_select",
    )(qp, sub_keys_b)


def _peer_body(xq_ref, e0_ref, d_ref, s1_ref, e1_ref, u_ref, vt_ref, h_ref, y_ref,
               acc_scr, wt_scr, *, nblocks, pieces):
    e = pl.program_id(1)
    nk = PEER_NKEYS
    eb, d = u_ref.shape
    tb = xq_ref.shape[0]
    per_block = eb // nk
    rows_a, rows_c = eb // pieces, d // pieces

    @pl.when(e == 0)
    def _():
        acc_scr[...] = jnp.zeros(acc_scr.shape, F32)

    blk = jnp.minimum(e, nblocks - 1)
    first_keys = pl.ds(pl.multiple_of(blk * per_block, per_block), per_block)

    def step(cur, prev):
        d_rows = [d_ref[h, first_keys, :] for h in range(PEER_HEADS)]
        e0_rows = [e0_ref[h, first_keys, :] for h in range(PEER_HEADS)]
        tw = min(MXU_DIM, tb)
        for t0 in range(0, tb, tw):
            toks = slice(t0, t0 + tw)
            for p in range(pieces):
                hid = lax.dot_general(u_ref[p * rows_a:(p + 1) * rows_a, :], xq_ref[toks, :], _NT,
                                      preferred_element_type=F32)
                for il in range(rows_a // nk):
                    i = p * (rows_a // nk) + il
                    for c in range(0, tw, LANES):
                        lanes = slice(t0 + c, t0 + c + LANES)
                        for j0 in range(0, nk, GATE_ROWS):
                            keys = slice(j0, j0 + GATE_ROWS)
                            gate = jnp.zeros((GATE_ROWS, LANES), F32)
                            for h in range(PEER_HEADS):
                                sel = s1_ref[h, keys, lanes] >= d_rows[h][i:i + 1, lanes]
                                gate = gate + jnp.where(sel, e0_rows[h][i:i + 1, lanes] * e1_ref[h, keys, lanes], 0.0)
                            act = _gelu(hid[il * nk + j0:il * nk + j0 + GATE_ROWS, c:c + LANES])
                            rows = slice(p * rows_a + il * nk + j0, p * rows_a + il * nk + j0 + GATE_ROWS)
                            wt_scr[cur, rows, lanes] = (gate * act).astype(BF16)
            for p in range(pieces):
                slab = slice(p * rows_c, (p + 1) * rows_c)
                acc_scr[slab, toks] += _dot(vt_ref[slab, :], wt_scr[prev, :, toks])

    step(0, 0)

    @pl.when(e == pl.num_programs(1) - 1)
    def _():
        y_ref[...] = h_ref[...] + acc_scr[...].T


def _peer(xq, sel, u_b, vt_b, h, tb, eb):
    t, d = xq.shape
    ne = u_b.shape[0]
    nk = PEER_NKEYS
    tb = min(tb, t)
    nblocks = ne // eb
    single = pl.Buffered(1)
    tile = pl.BlockSpec((PEER_HEADS, nk, tb), lambda i, e: (0, 0, i), pipeline_mode=single)
    return pl.pallas_call(
        functools.partial(_peer_body, nblocks=nblocks, pieces=BLOCKS['peer_pieces']),
        grid=(t // tb, nblocks),
        in_specs=[pl.BlockSpec((tb, d), lambda i, e: (i, 0), pipeline_mode=single),
                  tile, tile, tile, tile,
                  pl.BlockSpec((eb, d), lambda i, e: (jnp.minimum(e, nblocks - 1), 0)),
                  pl.BlockSpec((d, eb), lambda i, e: (0, e)),
                  pl.BlockSpec((tb, d), lambda i, e: (i, 0), pipeline_mode=single)],
        out_specs=pl.BlockSpec((tb, d), lambda i, e: (i, 0)),
        out_shape=jax.ShapeDtypeStruct((t, d), F32),
        scratch_shapes=[pltpu.VMEM((d, tb), F32), pltpu.VMEM((2, eb, tb), BF16)],
        compiler_params=_params("parallel", "arbitrary"),
        name="peer",
    )(xq, *sel, u_b, vt_b, h)


def _layer(x, k_past, v_past, h0_re, h0_im, lambda_init, w):
    b, s, d = x.shape
    t = b * s
    aw = ATT_HEADS * V_DIM
    x2 = x.reshape(t, d)
    qb, k, kb, v, vb, u = _in_proj(x2, w['g_mix'], w['w_in'], w['seg'], w['gq'], w['gk'], tb=BLOCKS['proj_tokens'])
    if k_past is None:
        assert b == 1
        o = _attn_prompt(w['bounded'], qb, kb, vb, w['lam'], w['g_sub'], lambda_init,
                         tq=BLOCKS['attn_queries'], tk=BLOCKS['attn_keys'], rt=BLOCKS['attn_rows'])
        h0_re = jnp.zeros((b, w['nstate']), F32)
        h0_im = h0_re
        ssm_tb = BLOCKS['ssm_time']
    else:
        past = k_past.shape[1]
        r3 = lambda a: a.reshape(b, s, aw)
        k_t = jnp.transpose(k_past, (0, 2, 3, 4, 1)).reshape(b, ATT_HEADS, 2 * QK_DIM, past)
        o = _attn_sample(r3(qb), k_t, v_past.reshape(b, past * ATT_HEADS, V_DIM), r3(kb),
                         vb.reshape(b, s, 2 * aw), w['lam'], w['g_sub'], lambda_init).reshape(t, aw)
        ssm_tb = s
    y, h_re, h_im = _ssm(u.reshape(b, s, -1), h0_re.reshape(b, -1), h0_im.reshape(b, -1),
                         w['bw'], w['cw'], w['pw'], w['d_skip'], tb=ssm_tb)
    hres, xq, qp = _post(y.reshape(t, -1), o, x2, w['w_glu'], w['b_glu'], w['g_ssm'], w['w_out'], w['g_ffn'],
                         w['w_query'], tb=BLOCKS['proj_tokens'])
    sel = _peer_select(qp, w['sub_keys'], tb=BLOCKS['select_tokens'])
    out = _peer(xq, sel, w['u_table'], w['v_table_t'], hres, tb=BLOCKS['peer_tokens'], eb=BLOCKS['peer_experts'])
    ngroups = w['nstate'] // SSM_STATE
    return (out.reshape(b, s, d), k.reshape(b, s, ATT_HEADS, 2, QK_DIM), v.reshape(b, s, ATT_HEADS, V_DIM),
            h_re.reshape(b, ngroups, SSM_STATE), h_im.reshape(b, ngroups, SSM_STATE))


def _prep_weights(l, g_mix, w_in, g_q, g_k, lam_q1, lam_k1, lam_q2, lam_k2, g_sub, a_re, a_im, log_dt, b_re,
                  b_im, c_re, c_im, d_skip, w_glu, b_glu, g_ssm, w_out, g_ffn, w_query, sub_keys, u_table,
                  v_table):
    aw = ATT_HEADS * V_DIM
    row = lambda a: a[l].reshape(1, -1)
    seg_id = jnp.arange(aw) // QK_DIM
    seg = jnp.where(seg_id[:, None] == seg_id[None, :], 1.0 / QK_DIM, 0.0).astype(BF16)
    bw, cw, pw = _ssm_params(a_re[l], a_im[l], log_dt[l], b_re[l], b_im[l], c_re[l], c_im[l])
    q_scale = QK_DIM ** -0.5 * LOG2_E
    score_bound = QK_DIM * q_scale * jnp.max(jnp.abs(g_q[l])) * jnp.max(jnp.abs(g_k[l])) * BF16_SLACK
    return dict(
        bounded=(score_bound <= MAX_UNSHIFTED_LOG2).astype(jnp.int32).reshape(1),
        g_mix=row(g_mix), w_in=w_in[l].astype(BF16), seg=seg,
        gq=jnp.tile(g_q[l], aw // QK_DIM).reshape(1, aw) * q_scale,
        gk=jnp.tile(g_k[l], aw // QK_DIM).reshape(1, aw),
        lam=(row(lam_q1), row(lam_k1), row(lam_q2), row(lam_k2)), g_sub=row(g_sub),
        bw=bw, cw=cw, pw=pw, d_skip=row(d_skip), nstate=a_re.shape[1] * a_re.shape[2],
        w_glu=w_glu[l].astype(BF16), b_glu=row(b_glu), g_ssm=row(g_ssm), w_out=w_out[l].astype(BF16),
        g_ffn=row(g_ffn), w_query=w_query[l].astype(BF16), sub_keys=sub_keys[l].astype(BF16),
        u_table=u_table[l].astype(BF16), v_table_t=v_table[l].T.astype(BF16))


def kernel(x_prompt, x_sample, cache_k, cache_v, state_ssm_re, state_ssm_im, g_mix, w_in, g_q, g_k, lam_q1, lam_k1, lam_q2, lam_k2, g_sub, a_re, a_im, log_dt, b_re, b_im, c_re, c_im, d_skip, w_glu, b_glu, g_ssm, w_out, g_ffn, w_query, sub_keys, u_table, v_table):
    depth = w_in.shape[0]
    yp, ys = x_prompt, x_sample
    outs = [[] for _ in range(8)]
    for l in range(depth):
        lambda_init = 0.8 - 0.6 * math.exp(-0.3 * l)
        w = _prep_weights(l, g_mix, w_in, g_q, g_k, lam_q1, lam_k1, lam_q2, lam_k2, g_sub, a_re, a_im, log_dt,
                          b_re, b_im, c_re, c_im, d_skip, w_glu, b_glu, g_ssm, w_out, g_ffn, w_query, sub_keys,
                          u_table, v_table)
        yp, kp, vp, hrp, hip = _layer(yp, None, None, None, None, lambda_init, w)
        ys, kk, vv, hrs, his = _layer(ys, cache_k[l], cache_v[l], state_ssm_re[l], state_ssm_im[l],
                                      lambda_init, w)
        for acc, val in zip(outs, (kp, vp, hrp, hip, kk, vv, hrs, his)):
            acc.append(val)
    return (yp, ys) + tuple(jnp.stack(o) for o in outs)
```
